```python
import math
import jax, jax.numpy as jnp
from jax import lax
import numpy as np

D_MODEL = 2048
BATCH = 4
SEQ = 4096
DEPTH = 2
DEC_BATCH = 128
DEC_SEQ = 1
PAST_LEN = 16384
PAGE_SIZE = 128

N_EVEN = (DEPTH + 1) // 2
N_ODD = DEPTH // 2
POOL_WINDOWS = (2, 4, 8, 16)
POOL_GROUPS = len(POOL_WINDOWS)
POOL_WIDTH = D_MODEL // 2
POOL_GW = POOL_WIDTH // POOL_GROUPS
POOL_BUF = max(POOL_WINDOWS) - 1
SG_CHUNK = 128
SG_GROUPS = 8
SG_WIDTH = D_MODEL // 2
SG_GW = SG_WIDTH // SG_GROUPS
AB_IN = POOL_WIDTH + 2 * SG_WIDTH
AB_OUT = POOL_WIDTH + SG_WIDTH
DIFF_HEADS = 8
DIFF_KV_HEADS = 2
DIFF_GROUP = DIFF_HEADS // DIFF_KV_HEADS
DIFF_DH = 64
DIFF_VD = 2 * DIFF_DH
DIFF_SCALE = DIFF_DH ** -0.5
MLA_HEADS = 8
MLA_Q_LORA = 512
MLA_KV_LORA = 512
MLA_NOPE = 128
MLA_ROPE = 64
MLA_VD = 128
MLA_SCALE = (MLA_NOPE + MLA_ROPE) ** -0.5
ROPE_BASE = 10000.0
CD_SPLITS = (DIFF_HEADS * 2 * DIFF_DH, DIFF_KV_HEADS * 2 * DIFF_DH, DIFF_KV_HEADS * DIFF_VD,
             MLA_Q_LORA, MLA_KV_LORA, MLA_ROPE)
CD_IN = sum(CD_SPLITS)
CD_OUT = DIFF_HEADS * DIFF_VD + MLA_HEADS * MLA_VD
REL_BUCKETS = 32
REL_MAX_DIST = 128
D_FF = 4 * D_MODEL
Q_BLOCK = 128
EPS = 1e-6
NEG_INF = -1e30

kernel_name = 'hybrid_pool_sgmlp_diffattn_mla_step'


def rmsnorm(x, g):
    xf = x.astype(jnp.float32)
    y = xf * lax.rsqrt(jnp.mean(xf * xf, axis=-1, keepdims=True) + EPS)
    return (y * g.astype(jnp.float32)).astype(x.dtype)


def sqrelu_mlp(h, w_up, w_down):
    a = jax.nn.relu(h @ w_up)
    return (a * a) @ w_down


def split_cols(p, sizes):
    out, off = [], 0
    for s in sizes:
        out.append(p[..., off:off + s])
        off += s
    return out


def rope(x, pos):
    half = x.shape[-1] // 2
    freq = ROPE_BASE ** (-jnp.arange(half, dtype=jnp.float32) / half)
    ang = pos.astype(jnp.float32)[:, None] * freq[None, :]
    cos, sin = jnp.cos(ang)[:, None, :], jnp.sin(ang)[:, None, :]
    xf = x.astype(jnp.float32)
    x1, x2 = xf[..., :half], xf[..., half:]
    return jnp.concatenate([x1 * cos - x2 * sin, x1 * sin + x2 * cos], axis=-1).astype(x.dtype)


def rel_bucket(qpos, kpos):
    n = jnp.maximum(qpos[:, None] - kpos[None, :], 0)
    exact = REL_BUCKETS // 2
    nf = jnp.maximum(n, 1).astype(jnp.float32)
    large = exact + (jnp.log(nf / exact) / math.log(REL_MAX_DIST / exact)
                     * (REL_BUCKETS - exact)).astype(jnp.int32)
    return jnp.where(n < exact, n, jnp.minimum(large, REL_BUCKETS - 1))


def rel_bias(table, qpos, kpos):
    b = jnp.transpose(table[rel_bucket(qpos, kpos)], (2, 0, 1)).astype(jnp.float32)
    return b.reshape(DIFF_KV_HEADS, 1, DIFF_GROUP, qpos.shape[0], kpos.shape[0])


def attn_partial(q, k, v, scale, bias, mask):
    s = jnp.einsum('bqhmgd,bkhmd->bhmgqk', q, k, preferred_element_type=jnp.float32) * scale
    if bias is not None:
        s = s + bias
    if mask is not None:
        s = jnp.where(mask, s, NEG_INF)
    m = jnp.max(s, axis=-1)
    p = jnp.exp(s - m[..., None])
    l = jnp.sum(p, axis=-1)
    o = jnp.einsum('bhmgqk,bkhv->bhmgqv', p, v.astype(jnp.float32))
    return m, l, o


def merge_partials(past, new):
    m, l, o = (jnp.concatenate([p, n[None]], axis=0) for p, n in zip(past, new))
    mx = jnp.max(m, axis=0)
    w = jnp.exp(m - mx)
    out = jnp.sum(w[..., None] * o, axis=0) / jnp.sum(w * l, axis=0)[..., None]
    return jnp.transpose(out, (0, 4, 1, 2, 3, 5))


def causal_attention(q, k, v, scale, bias_fn):
    T = q.shape[1]
    qb = min(T, Q_BLOCK)
    kpos = jnp.arange(T)

    def block(i):
        qs = lax.dynamic_slice_in_dim(q, i * qb, qb, axis=1)
        qpos = i * qb + jnp.arange(qb)
        bias = None if bias_fn is None else bias_fn(qpos, kpos)
        m, l, o = attn_partial(qs, k, v, scale, bias, qpos[:, None] >= kpos[None, :])
        return o / l[..., None]

    out = lax.map(block, jnp.arange(T // qb))
    out = jnp.moveaxis(out, 0, 4)
    out = out.reshape(out.shape[:4] + (T, out.shape[-1]))
    return jnp.transpose(out, (0, 4, 1, 2, 3, 5))


def multiscale_pool(z, buf, pos0, w_pool, scale):
    B, T, _ = z.shape
    full = z if buf is None else jnp.concatenate([buf.astype(z.dtype), z], axis=1)
    L = full.shape[1]
    zf = full.astype(jnp.float32).reshape(B, L, POOL_GROUPS, POOL_GW)
    pos = pos0 + jnp.arange(L)
    groups = []
    for gi, w in enumerate(POOL_WINDOWS):
        zg = zf[:, :, gi]
        cs = jnp.cumsum(jnp.pad(zg, ((0, 0), (w, 0), (0, 0))), axis=1)
        mean = (cs[:, w:] - cs[:, :L]) / jnp.minimum(pos + 1, w).astype(jnp.float32)[None, :, None]
        groups.append(mean - zg)
    pooled = jnp.stack(groups, axis=2)[:, L - T:]
    y = jnp.einsum('btgc,gcd->btgd', pooled, w_pool.astype(jnp.float32)).reshape(B, T, POOL_WIDTH)
    return y * scale.astype(jnp.float32), full[:, L - POOL_BUF:]


def spatial_gate(u, v, w_s, b_s):
    B, T, _ = u.shape
    c = min(T, SG_CHUNK)
    vc = v.reshape(B, T // c, c, SG_GROUPS, SG_GW)
    w = jnp.tril(w_s[:, :c, :c])
    mixed = jnp.einsum('gts,bnsgc->bntgc', w, vc) + b_s[:, :c].T[None, None, :, :, None]
    return u * mixed.reshape(B, T, SG_WIDTH).astype(u.dtype)


def mixer_ab(h, buf, pos0, w_in, w_pool, pool_scale, sg_norm, sg_w, sg_b, w_out):
    proj = h @ w_in
    z = proj[..., :POOL_WIDTH]
    uv = jax.nn.gelu(proj[..., POOL_WIDTH:])
    u = uv[..., :SG_WIDTH]
    v = rmsnorm(uv[..., SG_WIDTH:], sg_norm)
    ya, new_buf = multiscale_pool(z, buf, pos0, w_pool, pool_scale)
    yb = spatial_gate(u, v, sg_w, sg_b)
    y = jnp.concatenate([ya.astype(h.dtype), yb.astype(h.dtype)], axis=-1) @ w_out
    return y, new_buf, v


def cd_project(h, pos, w_in, q_norm, w_qb, kv_norm):
    B, T, _ = h.shape
    qd, kd, vd, cq, ckv, kpe = split_cols(h @ w_in, CD_SPLITS)
    qd = qd.reshape(B, T, DIFF_KV_HEADS, 2, DIFF_GROUP, DIFF_DH)
    kd = kd.reshape(B, T, DIFF_KV_HEADS, 2, DIFF_DH)
    vd = vd.reshape(B, T, DIFF_KV_HEADS, DIFF_VD)
    qm = (rmsnorm(cq, q_norm) @ w_qb).reshape(B, T, MLA_HEADS, MLA_NOPE + MLA_ROPE)
    q_nope, q_pe = qm[..., :MLA_NOPE], rope(qm[..., MLA_NOPE:], pos)
    ckv = rmsnorm(ckv, kv_norm)
    kpe = rope(kpe[:, :, None, :], pos)[:, :, 0, :]
    return qd, kd, vd, q_nope, q_pe, ckv, kpe


def diff_combine(o, lam_p, subln, li):
    lam_init = 0.8 - 0.6 * math.exp(-0.3 * li)
    lp = lam_p.astype(jnp.float32)
    lam = jnp.exp(jnp.sum(lp[0] * lp[1])) - jnp.exp(jnp.sum(lp[2] * lp[3])) + lam_init
    B, T = o.shape[:2]
    y = (o[:, :, :, 0] - lam * o[:, :, :, 1]).reshape(B, T, DIFF_HEADS, DIFF_VD)
    return (rmsnorm(y, subln) * (1.0 - lam_init)).reshape(B, T, DIFF_HEADS * DIFF_VD)


def mixer_cd_prompt(h, pos, li, w_in, lam_p, subln, q_norm, w_qb, kv_norm, w_kvb, w_out, rel_table):
    B, T, _ = h.shape
    qd, kd, vd, q_nope, q_pe, ckv, kpe = cd_project(h, pos, w_in, q_norm, w_qb, kv_norm)
    od = causal_attention(qd, kd, vd, DIFF_SCALE, lambda qp, kp: rel_bias(rel_table, qp, kp))
    yd = diff_combine(od, lam_p, subln, li)
    kvb = (ckv @ w_kvb).reshape(B, T, MLA_HEADS, MLA_NOPE + MLA_VD)
    k_full = jnp.concatenate([kvb[..., :MLA_NOPE],
                              jnp.broadcast_to(kpe[:, :, None, :], (B, T, MLA_HEADS, MLA_ROPE))], axis=-1)
    q_full = jnp.concatenate([q_nope, q_pe], axis=-1)
    om = causal_attention(q_full[:, :, :, None, None, :], k_full[:, :, :, None, :],
                          kvb[..., MLA_NOPE:], MLA_SCALE, None)
    ym = om.reshape(B, T, MLA_HEADS * MLA_VD)
    y = jnp.concatenate([yd.astype(h.dtype), ym.astype(h.dtype)], axis=-1) @ w_out
    return y, (kd.reshape(B, T, DIFF_KV_HEADS, 2 * DIFF_DH), vd, ckv, kpe)


def mixer_cd_sample(h, pos, li, a, page_table, c_dk, c_dv, c_ckv, c_kpe,
                    w_in, lam_p, subln, q_norm, w_qb, kv_norm, w_kvb, w_out, rel_table):
    B, T, _ = h.shape
    qd, kd, vd, q_nope, q_pe, ckv, kpe = cd_project(h, pos, w_in, q_norm, w_qb, kv_norm)
    w_kvb_h = w_kvb.reshape(MLA_KV_LORA, MLA_HEADS, MLA_NOPE + MLA_VD)
    w_uk, w_uv = w_kvb_h[..., :MLA_NOPE], w_kvb_h[..., MLA_NOPE:]
    q_lat = jnp.einsum('bthd,chd->bthc', q_nope, w_uk)
    qa = jnp.concatenate([q_lat, q_pe.astype(q_lat.dtype)], axis=-1)[:, :, None, None]

    def page_partials(j):
        phys = page_table[:, j]
        kd_p = c_dk[phys, a].reshape(B, PAGE_SIZE, DIFF_KV_HEADS, 2, DIFF_DH)
        vd_p = c_dv[phys, a]
        ckv_p = c_ckv[phys, a]
        ka_p = jnp.concatenate([ckv_p, c_kpe[phys, a]], axis=-1)
        kpos = j * PAGE_SIZE + jnp.arange(PAGE_SIZE)
        pd = attn_partial(qd, kd_p, vd_p, DIFF_SCALE, rel_bias(rel_table, pos, kpos), None)
        pm = attn_partial(qa, ka_p[:, :, None, None], ckv_p[:, :, None], MLA_SCALE, None, None)
        return pd + pm

    past = lax.map(page_partials, jnp.arange(page_table.shape[1]))
    mask = pos[:, None] >= pos[None, :]
    new_d = attn_partial(qd, kd, vd, DIFF_SCALE, rel_bias(rel_table, pos, pos), mask)
    ka = jnp.concatenate([ckv, kpe], axis=-1)
    new_m = attn_partial(qa, ka[:, :, None, None], ckv[:, :, None], MLA_SCALE, None, mask)
    od = merge_partials(past[:3], new_d)
    om = merge_partials(past[3:], new_m)
    yd = diff_combine(od, lam_p, subln, li)
    ym = jnp.einsum('bthc,chv->bthv', om.reshape(B, T, MLA_HEADS, MLA_KV_LORA), w_uv)
    ym = ym.reshape(B, T, MLA_HEADS * MLA_VD)
    y = jnp.concatenate([yd.astype(h.dtype), ym.astype(h.dtype)], axis=-1) @ w_out
    return y, (kd.reshape(B, T, DIFF_KV_HEADS, 2 * DIFF_DH), vd, ckv, kpe)


def setup_inputs(seed: int = 0) -> dict:
    key = jax.random.key(seed)
    ks = iter(jax.random.split(key, 32))

    def nrm(shape, scale):
        return jax.random.normal(next(ks), shape, jnp.float32) * scale

    def gain(shape):
        return 1.0 + 0.02 * jax.random.normal(next(ks), shape, jnp.float32)

    n_pages = PAST_LEN // PAGE_SIZE
    n_phys = (DEC_BATCH * n_pages * 5) // 4
    inputs = {}
    inputs['x_prompt'] = nrm((BATCH, SEQ, D_MODEL), 1.0)
    inputs['x_sample'] = nrm((DEC_BATCH, DEC_SEQ, D_MODEL), 1.0)
    inputs['state_pool'] = nrm((N_EVEN, DEC_BATCH, POOL_BUF, POOL_WIDTH), 1.0)
    inputs['cache_diff_k'] = nrm((n_phys, N_ODD, PAGE_SIZE, DIFF_KV_HEADS, 2 * DIFF_DH), 1.0)
    inputs['cache_diff_v'] = nrm((n_phys, N_ODD, PAGE_SIZE, DIFF_KV_HEADS, DIFF_VD), 1.0)
    inputs['cache_mla_ckv'] = nrm((n_phys, N_ODD, PAGE_SIZE, MLA_KV_LORA), 1.0)
    inputs['cache_mla_kpe'] = nrm((n_phys, N_ODD, PAGE_SIZE, MLA_ROPE), 1.0)
    inputs['page_table'] = jax.random.permutation(next(ks), n_phys)[: DEC_BATCH * n_pages].reshape(
        DEC_BATCH, n_pages).astype(jnp.int32)
    inputs['norm_mix'] = gain((DEPTH, D_MODEL))
    inputs['norm_ffn'] = gain((DEPTH, D_MODEL))
    inputs['norm_final'] = gain((D_MODEL,))
    inputs['w_ffn_up'] = nrm((DEPTH, D_MODEL, D_FF), D_MODEL ** -0.5)
    inputs['w_ffn_down'] = nrm((DEPTH, D_FF, D_MODEL), D_FF ** -0.5)
    inputs['w_in_ab'] = nrm((N_EVEN, D_MODEL, AB_IN), D_MODEL ** -0.5)
    inputs['pool_w'] = nrm((N_EVEN, POOL_GROUPS, POOL_GW, POOL_GW), POOL_GW ** -0.5)
    inputs['pool_scale'] = gain((N_EVEN, POOL_WIDTH))
    inputs['sg_norm'] = gain((N_EVEN, SG_WIDTH))
    inputs['sg_w'] = nrm((N_EVEN, SG_GROUPS, SG_CHUNK, SG_CHUNK), SG_CHUNK ** -0.5)
    inputs['sg_b'] = gain((N_EVEN, SG_GROUPS, SG_CHUNK))
    inputs['w_out_ab'] = nrm((N_EVEN, AB_OUT, D_MODEL), AB_OUT ** -0.5)
    inputs['w_in_cd'] = nrm((N_ODD, D_MODEL, CD_IN), D_MODEL ** -0.5)
    inputs['diff_lambda'] = nrm((N_ODD, 4, DIFF_DH), 0.1)
    inputs['diff_subln'] = gain((N_ODD, DIFF_VD))
    inputs['mla_q_norm'] = gain((N_ODD, MLA_Q_LORA))
    inputs['mla_w_qb'] = nrm((N_ODD, MLA_Q_LORA, MLA_HEADS * (MLA_NOPE + MLA_ROPE)), MLA_Q_LORA ** -0.5)
    inputs['mla_kv_norm'] = gain((N_ODD, MLA_KV_LORA))
    inputs['mla_w_kvb'] = nrm((N_ODD, MLA_KV_LORA, MLA_HEADS * (MLA_NOPE + MLA_VD)), MLA_KV_LORA ** -0.5)
    inputs['w_out_cd'] = nrm((N_ODD, CD_OUT, D_MODEL), CD_OUT ** -0.5)
    inputs['rel_table'] = nrm((REL_BUCKETS, DIFF_HEADS), 0.5)
    return inputs


def reference(x_prompt, x_sample, state_pool, cache_diff_k, cache_diff_v, cache_mla_ckv, cache_mla_kpe,
              page_table, norm_mix, norm_ffn, norm_final, w_ffn_up, w_ffn_down,
              w_in_ab, pool_w, pool_scale, sg_norm, sg_w, sg_b, w_out_ab,
              w_in_cd, diff_lambda, diff_subln, mla_q_norm, mla_w_qb, mla_kv_norm, mla_w_kvb, w_out_cd,
              rel_table):
    hp, hs = x_prompt, x_sample
    pos_p = jnp.arange(x_prompt.shape[1])
    pos_s = PAST_LEN + jnp.arange(x_sample.shape[1])
    pool_p, pool_s, sgv_s, rows_p, rows_s = [], [], [], [], []
    for li in range(DEPTH):
        e = li // 2
        n_p, n_s = rmsnorm(hp, norm_mix[li]), rmsnorm(hs, norm_mix[li])
        if li % 2 == 0:
            wts = (w_in_ab[e], pool_w[e], pool_scale[e], sg_norm[e], sg_w[e], sg_b[e], w_out_ab[e])
            yp, buf_p, _ = mixer_ab(n_p, None, 0, *wts)
            ys, buf_s, v_s = mixer_ab(n_s, state_pool[e], PAST_LEN - POOL_BUF, *wts)
            pool_p.append(buf_p)
            pool_s.append(buf_s)
            sgv_s.append(v_s)
        else:
            wts = (w_in_cd[e], diff_lambda[e], diff_subln[e], mla_q_norm[e], mla_w_qb[e],
                   mla_kv_norm[e], mla_w_kvb[e], w_out_cd[e], rel_table)
            yp, r_p = mixer_cd_prompt(n_p, pos_p, li, *wts)
            ys, r_s = mixer_cd_sample(n_s, pos_s, li, e, page_table, cache_diff_k, cache_diff_v,
                                      cache_mla_ckv, cache_mla_kpe, *wts)
            rows_p.append(r_p)
            rows_s.append(r_s)
        hp = hp + yp
        hs = hs + ys
        hp = hp + sqrelu_mlp(rmsnorm(hp, norm_ffn[li]), w_ffn_up[li], w_ffn_down[li])
        hs = hs + sqrelu_mlp(rmsnorm(hs, norm_ffn[li]), w_ffn_up[li], w_ffn_down[li])
    y_prompt = rmsnorm(hp, norm_final)
    y_sample = rmsnorm(hs, norm_final)
    pool_prompt = jnp.stack(pool_p, axis=0)
    pool_sample = jnp.stack(pool_s, axis=0)
    sgv_sample = jnp.stack(sgv_s, axis=0)
    dk_prompt, dv_prompt, ckv_prompt, kpe_prompt = (jnp.stack([r[i] for r in rows_p], axis=1) for i in range(4))
    dk_sample, dv_sample, ckv_sample, kpe_sample = (jnp.stack([r[i] for r in rows_s], axis=1) for i in range(4))
    return (y_prompt, y_sample, pool_prompt, pool_sample, sgv_sample,
            dk_prompt, dv_prompt, ckv_prompt, kpe_prompt,
            dk_sample, dv_sample, ckv_sample, kpe_sample)
```

```python
import functools
import math

import jax
import jax.numpy as jnp
from jax import lax
from jax.experimental import pallas as pl
from jax.experimental.pallas import tpu as pltpu

D_MODEL = 2048
PAGE_SIZE = 128
POOL_WINDOWS = (2, 4, 8, 16)
POOL_GROUPS = len(POOL_WINDOWS)
POOL_WIDTH = D_MODEL // 2
POOL_GW = POOL_WIDTH // POOL_GROUPS
POOL_BUF = max(POOL_WINDOWS) - 1
SG_CHUNK = 128
SG_GROUPS = 8
SG_WIDTH = D_MODEL // 2
SG_GW = SG_WIDTH // SG_GROUPS
DIFF_HEADS = 8
DIFF_KV_HEADS = 2
DIFF_GROUP = DIFF_HEADS // DIFF_KV_HEADS
DIFF_DH = 64
DIFF_VD = 2 * DIFF_DH
DIFF_SCALE = DIFF_DH ** -0.5
DIFF_MAPS = 2 * DIFF_HEADS
MLA_HEADS = 8
MLA_Q_LORA = 512
MLA_KV_LORA = 512
MLA_NOPE = 128
MLA_ROPE = 64
MLA_VD = 128
MLA_SCALE = (MLA_NOPE + MLA_ROPE) ** -0.5
ROPE_BASE = 10000.0
REL_BUCKETS = 32
REL_MAX_DIST = 128
D_FF = 4 * D_MODEL
EPS = 1e-6
NEG_INF = -1e30

LANES = 128
CD_Q = DIFF_HEADS * 2 * DIFF_DH
CD_K = DIFF_KV_HEADS * 2 * DIFF_DH
CD_V = DIFF_KV_HEADS * DIFF_VD
CD_IN_PAD = CD_Q + CD_K + CD_V + MLA_Q_LORA + MLA_KV_LORA + 2 * MLA_ROPE
MLA_QK = 2 * LANES

BF16 = jnp.bfloat16
F32 = jnp.float32
VMEM_LIMIT_MB = 56


def _rms(x, g):
    return x * lax.rsqrt(jnp.mean(x * x, axis=-1, keepdims=True) + EPS) * g


def _dot(a, b):
    return jnp.dot(a, b, preferred_element_type=F32)


def _dot_nt(a, b):
    return lax.dot_general(a, b, (((1,), (1,)), ((), ())), preferred_element_type=F32)


def _params(n_axes):
    return pltpu.CompilerParams(dimension_semantics=("arbitrary",) * n_axes,
                                vmem_limit_bytes=VMEM_LIMIT_MB * 2 ** 20)


def _full(shape):
    zeros = (0,) * len(shape)
    return pl.BlockSpec(shape, lambda *_: zeros)


def _ffn_kernel(h_ref, g_ref, wu_ref, wd_ref, gf_ref, o_ref, xn_ref, *, final_norm):
    f = pl.program_id(1)

    @pl.when(f == 0)
    def _():
        h = h_ref[...]
        xn_ref[...] = _rms(h, g_ref[...]).astype(BF16)
        o_ref[...] = h

    a = jnp.maximum(_dot(xn_ref[...], wu_ref[...]), 0.0)
    o_ref[...] += _dot((a * a).astype(BF16), wd_ref[...])

    if final_norm:
        @pl.when(f == pl.num_programs(1) - 1)
        def _():
            o_ref[...] = _rms(o_ref[...], gf_ref[...])


def _ffn(h, g, wu, wd, gf, final_norm):
    m = h.shape[0]
    tm, tf = min(m, 512), 1024
    return pl.pallas_call(
        functools.partial(_ffn_kernel, final_norm=final_norm),
        grid=(m // tm, D_FF // tf),
        in_specs=[pl.BlockSpec((tm, D_MODEL), lambda i, f: (i, 0)),
                  _full((1, D_MODEL)),
                  pl.BlockSpec((D_MODEL, tf), lambda i, f: (0, f)),
                  pl.BlockSpec((tf, D_MODEL), lambda i, f: (f, 0)),
                  _full((1, D_MODEL))],
        out_specs=pl.BlockSpec((tm, D_MODEL), lambda i, f: (i, 0)),
        out_shape=jax.ShapeDtypeStruct((m, D_MODEL), F32),
        scratch_shapes=[pltpu.VMEM((tm, D_MODEL), BF16)],
        compiler_params=_params(2), name="ffn")(h, g, wu, wd, gf)


def _mm_res_kernel(*refs, n):
    acc = refs[2 * n][...]
    for y_ref, w_ref in zip(refs[:n], refs[n:2 * n]):
        acc = acc + _dot(y_ref[...], w_ref[...])
    refs[2 * n + 1][...] = acc


def _mm_res(ys, ws, res):
    m = res.shape[0]
    tm = min(m, 512)
    n = len(ys)
    return pl.pallas_call(
        functools.partial(_mm_res_kernel, n=n),
        grid=(m // tm,),
        in_specs=([pl.BlockSpec((tm, y.shape[1]), lambda i: (i, 0)) for y in ys]
                  + [_full(w.shape) for w in ws]
                  + [pl.BlockSpec((tm, D_MODEL), lambda i: (i, 0))]),
        out_specs=pl.BlockSpec((tm, D_MODEL), lambda i: (i, 0)),
        out_shape=jax.ShapeDtypeStruct((m, D_MODEL), F32),
        compiler_params=_params(1), name="mm_res")(*ys, *ws, res)


def _ab_in_kernel(h_ref, g_ref, w_ref, sgn_ref, o_ref, xn_ref):
    j = pl.program_id(1)

    @pl.when(j == 0)
    def _():
        xn_ref[...] = _rms(h_ref[...], g_ref[...]).astype(BF16)

    p = _dot(xn_ref[...], w_ref[...])

    @pl.when(j == 0)
    def _():
        o_ref[...] = p

    @pl.when(j == 1)
    def _():
        o_ref[...] = jax.nn.gelu(p)

    @pl.when(j == 2)
    def _():
        o_ref[...] = _rms(jax.nn.gelu(p), sgn_ref[...])


def _ab_in(h, g, w, sgn):
    m = h.shape[0]
    tm = min(m, 512)
    return pl.pallas_call(
        _ab_in_kernel,
        grid=(m // tm, 3),
        in_specs=[pl.BlockSpec((tm, D_MODEL), lambda i, j: (i, 0)),
                  _full((1, D_MODEL)),
                  pl.BlockSpec((D_MODEL, POOL_WIDTH), lambda i, j: (0, j)),
                  _full((1, SG_WIDTH))],
        out_specs=pl.BlockSpec((tm, POOL_WIDTH), lambda i, j: (i, j)),
        out_shape=jax.ShapeDtypeStruct((m, 3 * POOL_WIDTH), F32),
        scratch_shapes=[pltpu.VMEM((tm, D_MODEL), BF16)],
        compiler_params=_params(2), name="ab_in")(h, g, w, sgn)


def _ab_mix_kernel(z_ref, u_ref, v_ref, pw_ref, ps_ref, sw_ref, sbt_ref, o_ref, zbuf_ref, *, tm):
    t = pl.program_id(1)
    carry = 16

    @pl.when(t == 0)
    def _():
        zbuf_ref[0:carry, :] = jnp.zeros((carry, POOL_WIDTH), F32)

    z = z_ref[...]
    zbuf_ref[carry:carry + tm, :] = z
    pos = t * tm + lax.broadcasted_iota(jnp.int32, (tm, 1), 0)
    for gi, w in enumerate(POOL_WINDOWS):
        lo, hi = gi * POOL_GW, (gi + 1) * POOL_GW
        s = z[:, lo:hi]
        for i in range(1, w):
            s = s + zbuf_ref[pl.ds(carry - i, tm), lo:hi]
        cnt = jnp.minimum(pos + 1, w).astype(F32)
        pooled = s / cnt - z[:, lo:hi]
        ya = _dot(pooled.astype(BF16), pw_ref[gi]) * ps_ref[:, lo:hi]
        o_ref[:, lo:hi] = ya.astype(BF16)
    zbuf_ref[0:carry, :] = zbuf_ref[tm:tm + carry, :]

    row = lax.broadcasted_iota(jnp.int32, (SG_CHUNK, SG_CHUNK), 0)
    col = lax.broadcasted_iota(jnp.int32, (SG_CHUNK, SG_CHUNK), 1)
    for g in range(SG_GROUPS):
        wg = jnp.where(row >= col, sw_ref[g], 0.0).astype(BF16)
        bg = sbt_ref[:, g:g + 1]
        lo, hi = g * SG_GW, (g + 1) * SG_GW
        for c in range(tm // SG_CHUNK):
            r0, r1 = c * SG_CHUNK, (c + 1) * SG_CHUNK
            mixed = _dot(wg, v_ref[r0:r1, lo:hi].astype(BF16)) + bg
            o_ref[r0:r1, POOL_WIDTH + lo:POOL_WIDTH + hi] = (u_ref[r0:r1, lo:hi] * mixed).astype(BF16)


def _ab_mix(zuv, batch, seq, pool_w, pool_scale, sg_w, sg_bt):
    tm = min(seq, 512)
    nt = seq // tm
    col = lambda c: pl.BlockSpec((tm, POOL_WIDTH), lambda b, t: (b * nt + t, c))
    return pl.pallas_call(
        functools.partial(_ab_mix_kernel, tm=tm),
        grid=(batch, nt),
        in_specs=[col(0), col(1), col(2),
                  _full(pool_w.shape), _full(pool_scale.shape), _full(sg_w.shape), _full(sg_bt.shape)],
        out_specs=pl.BlockSpec((tm, D_MODEL), lambda b, t: (b * nt + t, 0)),
        out_shape=jax.ShapeDtypeStruct((batch * seq, D_MODEL), BF16),
        scratch_shapes=[pltpu.VMEM((tm + 16, POOL_WIDTH), F32)],
        compiler_params=_params(2), name="ab_mix")(zuv, zuv, zuv, pool_w, pool_scale, sg_w, sg_bt)


def _ab_mix_s_kernel(zuv_ref, st_ref, pw_ref, ps_ref, w0_ref, b0_ref, o_ref, *, cnts):
    z = zuv_ref[:, 0:POOL_WIDTH]
    for gi, w in enumerate(POOL_WINDOWS):
        lo, hi = gi * POOL_GW, (gi + 1) * POOL_GW
        s = z[:, lo:hi]
        for i in range(1, w):
            s = s + st_ref[POOL_BUF - i, :, lo:hi]
        pooled = s / cnts[gi] - z[:, lo:hi]
        ya = _dot(pooled.astype(BF16), pw_ref[gi]) * ps_ref[:, lo:hi]
        o_ref[:, lo:hi] = ya.astype(BF16)
    u = zuv_ref[:, POOL_WIDTH:POOL_WIDTH + SG_WIDTH]
    v = zuv_ref[:, POOL_WIDTH + SG_WIDTH:]
    o_ref[:, POOL_WIDTH:] = (u * (w0_ref[...] * v + b0_ref[...])).astype(BF16)


def _ab_mix_s(zuv, state, pool_w, pool_scale, w0, b0, cnts):
    m = zuv.shape[0]
    return pl.pallas_call(
        functools.partial(_ab_mix_s_kernel, cnts=cnts),
        grid=(1,),
        in_specs=[_full(zuv.shape), _full(state.shape), _full(pool_w.shape), _full(pool_scale.shape),
                  _full(w0.shape), _full(b0.shape)],
        out_specs=_full((m, D_MODEL)),
        out_shape=jax.ShapeDtypeStruct((m, D_MODEL), BF16),
        compiler_params=_params(1), name="ab_mix_s")(zuv, state, pool_w, pool_scale, w0, b0)


def _rope128(t, cs):
    u = t * cs
    return u + pltpu.roll(u, 2 * 32, axis=1)


def _cd_in_kernel(h_ref, g_ref, win_ref, qn_ref, wqb_ref, kvn_ref, wkv_ref, cs_ref, *outs, sample):
    xn = _rms(h_ref[...], g_ref[...]).astype(BF16)
    p = _dot(xn, win_ref[...])
    tm = p.shape[0]
    lane = lax.broadcasted_iota(jnp.int32, (tm, LANES), 1)
    first = lane < MLA_ROPE
    cs = cs_ref[...]
    o_k, o_v = CD_Q, CD_Q + CD_K
    o_cq, o_ckv, o_pe = o_v + CD_V, o_v + CD_V + MLA_Q_LORA, o_v + CD_V + MLA_Q_LORA + MLA_KV_LORA
    dk, dv = p[:, o_k:o_v], p[:, o_v:o_cq]
    cqn = _rms(p[:, o_cq:o_ckv], qn_ref[...]).astype(BF16)
    ckv = _rms(p[:, o_ckv:o_pe], kvn_ref[...])
    rk = _rope128(p[:, o_pe:o_pe + LANES], cs)
    q = _dot(cqn, wqb_ref[...])
    q_nope = MLA_HEADS * MLA_NOPE
    if sample:
        qd_ref, dk_ref, dv_ref, ckv_ref, kpe_ref, qlat_ref, qpe_ref = outs
        qd_ref[...] = p[:, 0:CD_Q] * DIFF_SCALE
        for h in range(MLA_HEADS):
            qn = q[:, h * MLA_NOPE:(h + 1) * MLA_NOPE].astype(BF16)
            qlat_ref[:, h * MLA_KV_LORA:(h + 1) * MLA_KV_LORA] = _dot(qn, wkv_ref[h]).astype(BF16)
            qpe_ref[:, h * LANES:(h + 1) * LANES] = _rope128(
                q[:, q_nope + h * LANES:q_nope + (h + 1) * LANES], cs)
    else:
        qd_ref, dk_ref, dv_ref, dkb_ref, dvb_ref, ckv_ref, kpe_ref, qm_ref, km_ref, vm_ref = outs
        for hg in range(DIFF_HEADS):
            slot = p[:, hg * LANES:(hg + 1) * LANES] * DIFF_SCALE
            h, g = divmod(hg, DIFF_GROUP)
            qd_ref[0, (h * 2 + 0) * DIFF_GROUP + g] = jnp.where(first, slot, 0.0).astype(BF16)
            qd_ref[0, (h * 2 + 1) * DIFF_GROUP + g] = jnp.where(first, 0.0, slot).astype(BF16)
        dkb_ref[...] = dk.astype(BF16)
        dvb_ref[...] = dv.astype(BF16)
        kv = _dot(ckv.astype(BF16), wkv_ref[...])
        kpe_slot = jnp.where(first, rk, 0.0).astype(BF16)
        for h in range(MLA_HEADS):
            qm_ref[:, h * MLA_QK:h * MLA_QK + LANES] = q[:, h * MLA_NOPE:(h + 1) * MLA_NOPE].astype(BF16)
            qm_ref[:, h * MLA_QK + LANES:(h + 1) * MLA_QK] = _rope128(
                q[:, q_nope + h * LANES:q_nope + (h + 1) * LANES], cs).astype(BF16)
            km_ref[:, h * MLA_QK:h * MLA_QK + LANES] = kv[:, h * 2 * LANES:h * 2 * LANES + LANES].astype(BF16)
            km_ref[:, h * MLA_QK + LANES:(h + 1) * MLA_QK] = kpe_slot
            vm_ref[:, h * MLA_VD:(h + 1) * MLA_VD] = kv[:, h * 2 * LANES + LANES:(h + 1) * 2 * LANES].astype(BF16)
    dk_ref[...] = dk
    dv_ref[...] = dv
    ckv_ref[...] = ckv
    kpe_ref[...] = rk[:, 0:MLA_ROPE]


def _cd_in(h, g, win, qn, wqb, kvn, wkv, cs, batch, seq, sample):
    m = h.shape[0]
    tm = min(seq, 256) if not sample else m
    nt = seq // tm if not sample else 1
    row = lambda width, dt: (pl.BlockSpec((tm, width), lambda i: (i, 0)), jax.ShapeDtypeStruct((m, width), dt))
    common = [row(CD_K, F32), row(CD_V, F32)]
    tail = [row(MLA_KV_LORA, F32), row(MLA_ROPE, F32)]
    if sample:
        outs = ([row(CD_Q, F32)] + common + tail
                + [row(MLA_HEADS * MLA_KV_LORA, BF16), row(MLA_HEADS * LANES, F32)])
        cs_spec = _full(cs.shape)
    else:
        qd = (pl.BlockSpec((1, DIFF_MAPS, tm, LANES), lambda i: (i // nt, 0, i % nt, 0)),
              jax.ShapeDtypeStruct((batch, DIFF_MAPS, seq, LANES), BF16))
        outs = ([qd] + common + [row(CD_K, BF16), row(CD_V, BF16)] + tail
                + [row(MLA_HEADS * MLA_QK, BF16), row(MLA_HEADS * MLA_QK, BF16), row(MLA_HEADS * MLA_VD, BF16)])
        cs_spec = pl.BlockSpec((tm, LANES), lambda i: (i % nt, 0))
    return pl.pallas_call(
        functools.partial(_cd_in_kernel, sample=sample),
        grid=(m // tm,),
        in_specs=[pl.BlockSpec((tm, D_MODEL), lambda i: (i, 0)), _full(g.shape), _full(win.shape),
                  _full(qn.shape), _full(wqb.shape), _full(kvn.shape), _full(wkv.shape), cs_spec],
        out_specs=[o[0] for o in outs],
        out_shape=[o[1] for o in outs],
        compiler_params=_params(1), name="cd_in_s" if sample else "cd_in")(h, g, win, qn, wqb, kvn, wkv, cs)


def _bias_kernel(tab_ref, bd_ref, bp_ref, bs_ref, od_ref, op_ref, os_ref):
    def lookup(bkt, head):
        val = jnp.zeros(bkt.shape, F32)
        for b in range(REL_BUCKETS):
            val = jnp.where(bkt == b, tab_ref[b, head], val)
        return val

    bd, bp, bs = bd_ref[...], bp_ref[...], bs_ref[...]
    for head in range(DIFF_HEADS):
        far = tab_ref[REL_BUCKETS - 1, head]
        od_ref[head] = lookup(bd, head) - far
        op_ref[head] = lookup(bp, head) - far
        os_ref[head] = lookup(bs, head)


def _bias_tiles(rel_table, bkt_diag, bkt_prev, bkt_sample):
    return pl.pallas_call(
        _bias_kernel,
        in_specs=[pl.BlockSpec(memory_space=pltpu.SMEM),
                  pl.BlockSpec(memory_space=pltpu.VMEM), pl.BlockSpec(memory_space=pltpu.VMEM),
                  pl.BlockSpec(memory_space=pltpu.VMEM)],
        out_specs=[pl.BlockSpec(memory_space=pltpu.VMEM)] * 3,
        out_shape=[jax.ShapeDtypeStruct((DIFF_HEADS,) + bkt_diag.shape, F32),
                   jax.ShapeDtypeStruct((DIFF_HEADS,) + bkt_prev.shape, F32),
                   jax.ShapeDtypeStruct((DIFF_HEADS,) + bkt_sample.shape, F32)],
        name="rel_bias")(rel_table, bkt_diag, bkt_prev, bkt_sample)


def _softmax_step(s, v, m_ref, l_ref, acc_ref):
    m_prev = m_ref[...]
    m_new = jnp.maximum(m_prev, jnp.max(s, axis=-1, keepdims=True))
    alpha = jnp.exp(m_prev - m_new)
    p = jnp.exp(s - m_new)
    l_ref[...] = alpha * l_ref[...] + jnp.sum(p, axis=-1, keepdims=True)
    acc_ref[...] = alpha * acc_ref[...] + _dot(p.astype(BF16), v)
    m_ref[...] = m_new


def _softmax_init(m_ref, l_ref, acc_ref):
    m_ref[...] = jnp.full(m_ref.shape, NEG_INF, F32)
    l_ref[...] = jnp.zeros(l_ref.shape, F32)
    acc_ref[...] = jnp.zeros(acc_ref.shape, F32)


def _diff_lambda(lp, li):
    lam_init = 0.8 - 0.6 * math.exp(-0.3 * li)
    a = jnp.sum(lp[0:1, :] * lp[1:2, :], axis=-1, keepdims=True)
    b = jnp.sum(lp[2:3, :] * lp[3:4, :], axis=-1, keepdims=True)
    return jnp.exp(a) - jnp.exp(b) + lam_init, lam_init


def _diff_attn_kernel(q_ref, k_ref, v_ref, bd_ref, bp_ref, lam_ref, sub_ref, o_ref,
                      m_ref, l_ref, acc_ref, *, tb, li):
    i, j = pl.program_id(2), pl.program_id(3)
    rows = 2 * DIFF_GROUP

    @pl.when(j == 0)
    def _():
        _softmax_init(m_ref, l_ref, acc_ref)

    def step(b_ref, masked):
        s = _dot_nt(q_ref[0].reshape(rows * tb, LANES), k_ref[...])
        if b_ref is not None:
            s = (s.reshape(2, DIFF_GROUP, tb, tb) + b_ref[...][None]).reshape(rows, tb, tb)
            if masked:
                r = lax.broadcasted_iota(jnp.int32, (tb, tb), 0)
                c = lax.broadcasted_iota(jnp.int32, (tb, tb), 1)
                s = jnp.where((r >= c)[None], s, NEG_INF)
            s = s.reshape(rows * tb, tb)
        _softmax_step(s, v_ref[...], m_ref, l_ref, acc_ref)

    @pl.when(j < i - 1)
    def _():
        step(None, False)

    @pl.when(j == i - 1)
    def _():
        step(bp_ref, False)

    @pl.when(j == i)
    def _():
        step(bd_ref, True)
        o = (acc_ref[...] / l_ref[...]).reshape(2, DIFF_GROUP, tb, DIFF_VD)
        lam, lam_init = _diff_lambda(lam_ref[...], li)
        for g in range(DIFF_GROUP):
            y = o[0, g] - lam * o[1, g]
            o_ref[:, g * DIFF_VD:(g + 1) * DIFF_VD] = (_rms(y, sub_ref[...]) * (1.0 - lam_init)).astype(BF16)


def _diff_attn(qd, kb, vb, bias_d, bias_p, lam_p, subln, batch, seq, tb, li):
    n = seq // tb
    rows = 2 * DIFF_GROUP
    kv = pl.BlockSpec((tb, LANES), lambda b, h, i, j: (b * n + jnp.minimum(j, i), h))
    bias = pl.BlockSpec((DIFF_GROUP, tb, tb), lambda b, h, i, j: (h, 0, 0))
    return pl.pallas_call(
        functools.partial(_diff_attn_kernel, tb=tb, li=li),
        grid=(batch, DIFF_KV_HEADS, n, n),
        in_specs=[pl.BlockSpec((1, rows, tb, LANES), lambda b, h, i, j: (b, h, i, 0)),
                  kv, kv, bias, bias, _full(lam_p.shape), _full(subln.shape)],
        out_specs=pl.BlockSpec((tb, DIFF_GROUP * DIFF_VD), lambda b, h, i, j: (b * n + i, h)),
        out_shape=jax.ShapeDtypeStruct((batch * seq, DIFF_HEADS * DIFF_VD), BF16),
        scratch_shapes=[pltpu.VMEM((rows * tb, 1), F32), pltpu.VMEM((rows * tb, 1), F32),
                        pltpu.VMEM((rows * tb, DIFF_VD), F32)],
        compiler_params=_params(4), name="diff_attn")(qd, kb, vb, bias_d, bias_p, lam_p, subln)


def _mla_attn_kernel(q_ref, k_ref, v_ref, o_ref, m_ref, l_ref, acc_ref, *, tq, tk):
    i, j = pl.program_id(2), pl.program_id(3)
    ratio = tq // tk

    @pl.when(j == 0)
    def _():
        _softmax_init(m_ref, l_ref, acc_ref)

    def step(masked):
        s = _dot_nt(q_ref[...], k_ref[...]) * MLA_SCALE
        if masked:
            r = i * tq + lax.broadcasted_iota(jnp.int32, (tq, tk), 0)
            c = j * tk + lax.broadcasted_iota(jnp.int32, (tq, tk), 1)
            s = jnp.where(r >= c, s, NEG_INF)
        _softmax_step(s, v_ref[...], m_ref, l_ref, acc_ref)

    @pl.when(j < ratio * i)
    def _():
        step(False)

    @pl.when(jnp.logical_and(j >= ratio * i, j < ratio * (i + 1)))
    def _():
        step(True)

    @pl.when(j == ratio * (i + 1) - 1)
    def _():
        o_ref[...] = (acc_ref[...] / l_ref[...]).astype(BF16)


def _mla_attn(qm, km, vm, batch, seq):
    tq, tk = min(seq, 1024), min(seq, 512)
    nq, nk = seq // tq, seq // tk
    ratio = tq // tk
    kmap = lambda b, h, i, j: (b * nk + jnp.minimum(j, ratio * (i + 1) - 1), h)
    return pl.pallas_call(
        functools.partial(_mla_attn_kernel, tq=tq, tk=tk),
        grid=(batch, MLA_HEADS, nq, nk),
        in_specs=[pl.BlockSpec((tq, MLA_QK), lambda b, h, i, j: (b * nq + i, h)),
                  pl.BlockSpec((tk, MLA_QK), kmap),
                  pl.BlockSpec((tk, MLA_VD), kmap)],
        out_specs=pl.BlockSpec((tq, MLA_VD), lambda b, h, i, j: (b * nq + i, h)),
        out_shape=jax.ShapeDtypeStruct((batch * seq, MLA_HEADS * MLA_VD), BF16),
        scratch_shapes=[pltpu.VMEM((tq, 1), F32), pltpu.VMEM((tq, 1), F32), pltpu.VMEM((tq, MLA_VD), F32)],
        compiler_params=_params(4), name="mla_attn")(qm, km, vm)


def _decode_kernel(pt_ref, qd_ref, ql_ref, qp_ref, bias_ref, bnew_ref, kdn_ref, vdn_ref, ckvn_ref, kpen_ref,
                   *rest, pages):
    kd_refs, vd_refs = rest[0:pages], rest[pages:2 * pages]
    ckv_refs, kpe_refs = rest[2 * pages:3 * pages], rest[3 * pages:4 * pages]
    od_ref, om_ref, md_ref, ld_ref, accd_ref, mm_ref, lm_ref, accm_ref = rest[4 * pages:]
    del pt_ref
    c = pl.program_id(1)

    @pl.when(c == 0)
    def _():
        _softmax_init(md_ref, ld_ref, accd_ref)
        _softmax_init(mm_ref, lm_ref, accm_ref)

    qd, ql, qp = qd_ref[0], ql_ref[0], qp_ref[0]
    vds = [r[...].astype(BF16) for r in vd_refs]
    ckvs = [r[0].astype(BF16) for r in ckv_refs]
    sd = jnp.concatenate([_dot_nt(qd, r[...].astype(BF16)) for r in kd_refs], axis=1) + bias_ref[...]
    sm = jnp.concatenate([_dot_nt(ql, ckv) + _dot(qp, r[0].astype(BF16))
                          for ckv, r in zip(ckvs, kpe_refs)], axis=1) * MLA_SCALE

    def update(s, vals, m_ref, l_ref, acc_ref):
        m_prev = m_ref[...]
        m_new = jnp.maximum(m_prev, jnp.max(s, axis=-1, keepdims=True))
        alpha = jnp.exp(m_prev - m_new)
        p = jnp.exp(s - m_new)
        l_ref[...] = alpha * l_ref[...] + jnp.sum(p, axis=-1, keepdims=True)
        pb = p.astype(BF16)
        acc = alpha * acc_ref[...]
        width = vals[0].shape[0]
        for k, val in enumerate(vals):
            acc = acc + _dot(pb[:, k * width:(k + 1) * width], val)
        acc_ref[...] = acc
        m_ref[...] = m_new

    update(sd, vds, md_ref, ld_ref, accd_ref)
    update(sm, ckvs, mm_ref, lm_ref, accm_ref)

    @pl.when(c == pl.num_programs(1) - 1)
    def _():
        def finish(s_new, v_new, m_ref, l_ref, acc_ref, o_ref):
            m_prev = m_ref[...]
            m_new = jnp.maximum(m_prev, s_new)
            alpha = jnp.exp(m_prev - m_new)
            p = jnp.exp(s_new - m_new)
            l = alpha * l_ref[...] + p
            o_ref[0] = (alpha * acc_ref[...] + p * v_new) / l

        head0 = lax.broadcasted_iota(jnp.int32, (DIFF_MAPS, LANES), 0) < 2 * DIFF_GROUP
        kdn, vdn = kdn_ref[0], vdn_ref[0]
        kd_new = jnp.where(head0, kdn[:, 0:LANES], kdn[:, LANES:])
        vd_new = jnp.where(head0, vdn[:, 0:DIFF_VD], vdn[:, DIFF_VD:])
        sd_new = jnp.sum(qd.astype(F32) * kd_new, axis=-1, keepdims=True) + bnew_ref[:, 0:1]
        finish(sd_new, vd_new, md_ref, ld_ref, accd_ref, od_ref)
        sm_new = (jnp.sum(ql.astype(F32) * ckvn_ref[0], axis=-1, keepdims=True)
                  + jnp.sum(qp.astype(F32) * kpen_ref[0], axis=-1, keepdims=True)) * MLA_SCALE
        finish(sm_new, ckvn_ref[0], mm_ref, lm_ref, accm_ref, om_ref)


def _decode(page_idx, qd, ql, qp, bias, bias_new, kd_new, vd_new, ckv_new, kpe_new, c_dk, c_dv, c_ckv, c_kpe,
            n_pages):
    nb = qd.shape[0]
    pages = 8 if n_pages % 8 == 0 else 1
    nch = n_pages // pages
    rows = DIFF_MAPS
    drows = PAGE_SIZE * DIFF_KV_HEADS
    per_b = lambda shape: pl.BlockSpec((1,) + shape, lambda b, c, pt: (b, 0, 0))
    pidx = lambda b, c, pt, k: pt[b * n_pages + c * pages + k]

    def page2(k):
        return pl.BlockSpec((drows, LANES), lambda b, c, pt: (pidx(b, c, pt, k), 0))

    def page3(shape, k):
        return pl.BlockSpec((1,) + shape, lambda b, c, pt: (pidx(b, c, pt, k), 0, 0))

    in_specs = ([per_b((rows, LANES)), per_b((rows, MLA_KV_LORA)), per_b((rows, MLA_ROPE)),
                 pl.BlockSpec((rows, pages * drows), lambda b, c, pt: (0, c)),
                 _full(bias_new.shape),
                 per_b((1, CD_K)), per_b((1, CD_V)), per_b((1, MLA_KV_LORA)), per_b((1, MLA_ROPE))]
                + [page2(k) for k in range(pages)] + [page2(k) for k in range(pages)]
                + [page3((PAGE_SIZE, MLA_KV_LORA), k) for k in range(pages)]
                + [page3((MLA_ROPE, PAGE_SIZE), k) for k in range(pages)])
    grid_spec = pltpu.PrefetchScalarGridSpec(
        num_scalar_prefetch=1, grid=(nb, nch), in_specs=in_specs,
        out_specs=[per_b((rows, DIFF_VD)), per_b((rows, MLA_KV_LORA))],
        scratch_shapes=[pltpu.VMEM((rows, 1), F32), pltpu.VMEM((rows, 1), F32), pltpu.VMEM((rows, DIFF_VD), F32),
                        pltpu.VMEM((rows, 1), F32), pltpu.VMEM((rows, 1), F32),
                        pltpu.VMEM((rows, MLA_KV_LORA), F32)])
    return pl.pallas_call(
        functools.partial(_decode_kernel, pages=pages),
        grid_spec=grid_spec,
        out_shape=[jax.ShapeDtypeStruct((nb, rows, DIFF_VD), F32),
                   jax.ShapeDtypeStruct((nb, rows, MLA_KV_LORA), F32)],
        compiler_params=_params(2), name="decode_attn")(
            page_idx, qd, ql, qp, bias, bias_new, kd_new, vd_new, ckv_new, kpe_new,
            *([c_dk] * pages), *([c_dv] * pages), *([c_ckv] * pages), *([c_kpe] * pages))


def _sample_out_kernel(o0_ref, o1_ref, om_ref, lam_ref, sub_ref, wuv_ref, w1_ref, w2_ref, res_ref, o_ref, *, li):
    lam, lam_init = _diff_lambda(lam_ref[...], li)
    acc = res_ref[...]
    for hd in range(DIFF_HEADS):
        lo, hi = hd * DIFF_VD, (hd + 1) * DIFF_VD
        y = o0_ref[:, lo:hi] - lam * o1_ref[:, lo:hi]
        yd = (_rms(y, sub_ref[...]) * (1.0 - lam_init)).astype(BF16)
        acc = acc + _dot(yd, w1_ref[lo:hi, :])
    for h in range(MLA_HEADS):
        om = om_ref[:, h * MLA_KV_LORA:(h + 1) * MLA_KV_LORA].astype(BF16)
        ym = _dot(om, wuv_ref[h]).astype(BF16)
        acc = acc + _dot(ym, w2_ref[h * MLA_VD:(h + 1) * MLA_VD, :])
    o_ref[...] = acc


def _sample_out(o0, o1, om, lam_p, subln, wuv, w1, w2, res, li):
    args = (o0, o1, om, lam_p, subln, wuv, w1, w2, res)
    return pl.pallas_call(
        functools.partial(_sample_out_kernel, li=li),
        grid=(1,),
        in_specs=[_full(a.shape) for a in args],
        out_specs=_full(res.shape),
        out_shape=jax.ShapeDtypeStruct(res.shape, F32),
        compiler_params=_params(1), name="sample_out")(*args)


def _rope_table(pos):
    half = MLA_ROPE // 2
    freq = ROPE_BASE ** (-jnp.arange(half, dtype=F32) / half)
    ang = pos.astype(F32)[:, None] * freq[None, :]
    c, s = jnp.cos(ang), jnp.sin(ang)
    return jnp.concatenate([c, c, -s, s], axis=-1)


def _rel_bucket(n):
    exact = REL_BUCKETS // 2
    nf = jnp.maximum(n, 1).astype(F32)
    large = exact + (jnp.log(nf / exact) / math.log(REL_MAX_DIST / exact)
                     * (REL_BUCKETS - exact)).astype(jnp.int32)
    return jnp.where(n < exact, n, jnp.minimum(large, REL_BUCKETS - 1))


def _row2(x):
    return x.reshape(1, -1)


def kernel(x_prompt, x_sample, state_pool, cache_diff_k, cache_diff_v, cache_mla_ckv, cache_mla_kpe, page_table, norm_mix, norm_ffn, norm_final, w_ffn_up, w_ffn_down, w_in_ab, pool_w, pool_scale, sg_norm, sg_w, sg_b, w_out_ab, w_in_cd, diff_lambda, diff_subln, mla_q_norm, mla_w_qb, mla_kv_norm, mla_w_kvb, w_out_cd, rel_table):
    batch, seq, _ = x_prompt.shape
    nb, dec_seq, _ = x_sample.shape
    assert dec_seq == 1
    n_pages = page_table.shape[1]
    past_len = n_pages * PAGE_SIZE
    depth = norm_mix.shape[0]
    n_odd = cache_diff_k.shape[1]
    hp = x_prompt.reshape(batch * seq, D_MODEL)
    hs = x_sample.reshape(nb, D_MODEL)
    gf = _row2(norm_final)

    pool_p, pool_s, sgv_s, rows_p, rows_s = [], [], [], [], []
    for li in range(depth):
        e = li // 2
        g_mix, g_ffn = _row2(norm_mix[li]), _row2(norm_ffn[li])
        if li % 2 == 0:
            w_in = w_in_ab[e].astype(BF16)
            pw = pool_w[e].astype(BF16)
            ps = _row2(pool_scale[e])
            sgn = _row2(sg_norm[e])
            w_out = w_out_ab[e].astype(BF16)
            zuv = _ab_in(hp, g_mix, w_in, sgn)
            yab = _ab_mix(zuv, batch, seq, pw, ps, sg_w[e], sg_b[e, :, :SG_CHUNK].T)
            pool_p.append(zuv[:, :POOL_WIDTH].reshape(batch, seq, POOL_WIDTH)[:, seq - POOL_BUF:])
            hp = _mm_res([yab], [w_out], hp)
            zuv_s = _ab_in(hs, g_mix, w_in, sgn)
            w0 = _row2(jnp.repeat(sg_w[e, :, 0, 0], SG_GW))
            b0 = _row2(jnp.repeat(sg_b[e, :, 0], SG_GW))
            cnts = tuple(float(min(past_len + 1, w)) for w in POOL_WINDOWS)
            yab_s = _ab_mix_s(zuv_s, jnp.transpose(state_pool[e], (1, 0, 2)), pw, ps, w0, b0, cnts)
            pool_s.append(jnp.concatenate([state_pool[e][:, 1:], zuv_s[:, None, :POOL_WIDTH]], axis=1))
            sgv_s.append(zuv_s[:, None, POOL_WIDTH + SG_WIDTH:])
            hs = _mm_res([yab_s], [w_out], hs)
        else:
            w = w_in_cd[e]
            wq = w[:, :CD_Q].reshape(D_MODEL, DIFF_KV_HEADS, 2, DIFF_GROUP, DIFF_DH)
            wq = jnp.transpose(wq, (0, 1, 3, 2, 4)).reshape(D_MODEL, CD_Q)
            half = MLA_ROPE // 2
            w_pe = w[:, -MLA_ROPE:]
            w_in = jnp.concatenate([wq, w[:, CD_Q:], w_pe[:, half:], w_pe[:, :half]], axis=1).astype(BF16)
            wqb = mla_w_qb[e].reshape(MLA_Q_LORA, MLA_HEADS, MLA_NOPE + MLA_ROPE)
            wq_pe = wqb[..., MLA_NOPE:]
            wq_pe = jnp.concatenate([wq_pe, wq_pe[..., half:], wq_pe[..., :half]], axis=-1)
            wqb = jnp.concatenate([wqb[..., :MLA_NOPE].reshape(MLA_Q_LORA, -1),
                                   wq_pe.reshape(MLA_Q_LORA, -1)], axis=1).astype(BF16)
            wkvb = mla_w_kvb[e].astype(BF16)
            wkvb_h = mla_w_kvb[e].reshape(MLA_KV_LORA, MLA_HEADS, MLA_NOPE + MLA_VD)
            w_uk_t = jnp.transpose(wkvb_h[..., :MLA_NOPE], (1, 2, 0)).astype(BF16)
            w_uv = jnp.transpose(wkvb_h[..., MLA_NOPE:], (1, 0, 2)).astype(BF16)
            w_out = w_out_cd[e].astype(BF16)
            w_out1, w_out2 = w_out[:DIFF_HEADS * DIFF_VD], w_out[DIFF_HEADS * DIFF_VD:]
            qn, kvn = _row2(mla_q_norm[e]), _row2(mla_kv_norm[e])
            lam_p, subln = diff_lambda[e], _row2(diff_subln[e])

            tb = min(seq, 256)
            r = jnp.arange(tb)
            dist = r[:, None] - r[None, :]
            n_keys = past_len + LANES
            kpos = jnp.arange(n_keys)
            dist_s = jnp.where(kpos < past_len, past_len - kpos, 0)
            bias_d, bias_p, bias_s = _bias_tiles(
                rel_table, _rel_bucket(jnp.maximum(dist, 0)), _rel_bucket(dist + tb),
                jnp.broadcast_to(_rel_bucket(dist_s)[None], (8, n_keys)))

            cs = _rope_table(jnp.arange(seq))
            (qd, dk, dv, dkb, dvb, ckv, kpe, qm, km, vm) = _cd_in(
                hp, g_mix, w_in, qn, wqb, kvn, wkvb, cs, batch, seq, False)
            yd = _diff_attn(qd, dkb, dvb, bias_d, bias_p, lam_p, subln, batch, seq, tb, li)
            ym = _mla_attn(qm, km, vm, batch, seq)
            hp = _mm_res([yd, ym], [w_out1, w_out2], hp)
            rows_p.append((dk.reshape(batch, seq, DIFF_KV_HEADS, 2 * DIFF_DH),
                           dv.reshape(batch, seq, DIFF_KV_HEADS, DIFF_VD),
                           ckv.reshape(batch, seq, MLA_KV_LORA), kpe.reshape(batch, seq, MLA_ROPE)))

            cs_s = jnp.broadcast_to(_rope_table(jnp.full((1,), past_len)), (nb, LANES))
            (qd_s, dk_s, dv_s, ckv_s, kpe_s, qlat, qpe) = _cd_in(
                hs, g_mix, w_in, qn, wqb, kvn, w_uk_t, cs_s, nb, 1, True)
            row_id = jnp.arange(DIFF_MAPS)
            qd_r = jnp.transpose(qd_s.reshape(nb, DIFF_KV_HEADS, DIFF_GROUP, 2, DIFF_DH), (0, 1, 3, 2, 4))
            qd_r = qd_r.reshape(nb, DIFF_MAPS, 1, DIFF_DH)
            slot = ((row_id // DIFF_GROUP) % 2)[:, None] == jnp.arange(2)[None, :]
            qd_pad = jnp.where(slot[None, :, :, None], qd_r, 0.0).reshape(nb, DIFF_MAPS, LANES).astype(BF16)
            pad = ((0, 0), (0, DIFF_MAPS - MLA_HEADS), (0, 0))
            ql = jnp.pad(qlat.reshape(nb, MLA_HEADS, MLA_KV_LORA), pad)
            qp = jnp.pad(qpe.reshape(nb, MLA_HEADS, LANES)[:, :, :MLA_ROPE], pad).astype(BF16)
            row_head = row_id // (2 * DIFF_GROUP)
            bias_rows = bias_s[row_head * DIFF_GROUP + row_id % DIFF_GROUP, 0]
            own = row_head[:, None] == (jnp.arange(past_len * DIFF_KV_HEADS) % DIFF_KV_HEADS)[None, :]
            bias_past = jnp.where(own, jnp.repeat(bias_rows[:, :past_len], DIFF_KV_HEADS, axis=1), NEG_INF)
            od, om = _decode(
                page_table.reshape(-1) * n_odd + e, qd_pad, ql, qp, bias_past, bias_rows[:, past_len:],
                dk_s[:, None], dv_s[:, None], ckv_s[:, None], kpe_s[:, None],
                cache_diff_k.reshape(-1, LANES), cache_diff_v.reshape(-1, LANES),
                cache_mla_ckv.reshape(-1, PAGE_SIZE, MLA_KV_LORA),
                jnp.swapaxes(cache_mla_kpe, 2, 3).reshape(-1, MLA_ROPE, PAGE_SIZE), n_pages)
            od = od.reshape(nb, DIFF_KV_HEADS, 2, DIFF_GROUP * DIFF_VD)
            o_map = [od[:, :, mp].reshape(nb, -1) for mp in range(2)]
            hs = _sample_out(o_map[0], o_map[1], om[:, :MLA_HEADS].reshape(nb, -1), lam_p, subln, w_uv,
                             w_out1, w_out2, hs, li)
            rows_s.append((dk_s.reshape(nb, 1, DIFF_KV_HEADS, 2 * DIFF_DH),
                           dv_s.reshape(nb, 1, DIFF_KV_HEADS, DIFF_VD),
                           ckv_s.reshape(nb, 1, MLA_KV_LORA), kpe_s.reshape(nb, 1, MLA_ROPE)))
        last = li == depth - 1
        hp = _ffn(hp, g_ffn, w_ffn_up[li].astype(BF16), w_ffn_down[li].astype(BF16), gf, last)
        hs = _ffn(hs, g_ffn, w_ffn_up[li].astype(BF16), w_ffn_down[li].astype(BF16), gf, last)

    y_prompt = hp.reshape(batch, seq, D_MODEL)
    y_sample = hs.reshape(nb, 1, D_MODEL)
    stack = lambda rows, i: jnp.stack([r[i] for r in rows], axis=1)
    return (y_prompt, y_sample, jnp.stack(pool_p, axis=0), jnp.stack(pool_s, axis=0), jnp.stack(sgv_s, axis=0),
            stack(rows_p, 0), stack(rows_p, 1), stack(rows_p, 2), stack(rows_p, 3),
            stack(rows_s, 0), stack(rows_s, 1), stack(rows_s, 2), stack(rows_s, 3))
```

```python
import functools
import math

import jax
import jax.numpy as jnp
from jax import lax
from jax.experimental import pallas as pl
from jax.experimental.pallas import tpu as pltpu

D_MODEL = 2048
PAGE_SIZE = 128
POOL_WINDOWS = (2, 4, 8, 16)
POOL_GROUPS = len(POOL_WINDOWS)
POOL_WIDTH = D_MODEL // 2
POOL_GW = POOL_WIDTH // POOL_GROUPS
POOL_BUF = max(POOL_WINDOWS) - 1
SG_CHUNK = 128
SG_GROUPS = 8
SG_WIDTH = D_MODEL // 2
SG_GW = SG_WIDTH // SG_GROUPS
DIFF_HEADS = 8
DIFF_KV_HEADS = 2
DIFF_GROUP = DIFF_HEADS // DIFF_KV_HEADS
DIFF_DH = 64
DIFF_VD = 2 * DIFF_DH
DIFF_SCALE = DIFF_DH ** -0.5
DIFF_MAPS = 2 * DIFF_HEADS
MLA_HEADS = 8
MLA_Q_LORA = 512
MLA_KV_LORA = 512
MLA_NOPE = 128
MLA_ROPE = 64
MLA_VD = 128
MLA_SCALE = (MLA_NOPE + MLA_ROPE) ** -0.5
ROPE_BASE = 10000.0
REL_BUCKETS = 32
REL_MAX_DIST = 128
D_FF = 4 * D_MODEL
EPS = 1e-6
NEG_INF = -1e30

LANES = 128
CD_Q = DIFF_HEADS * 2 * DIFF_DH
CD_K = DIFF_KV_HEADS * 2 * DIFF_DH
CD_V = DIFF_KV_HEADS * DIFF_VD
CD_IN_PAD = CD_Q + CD_K + CD_V + MLA_Q_LORA + MLA_KV_LORA + 2 * MLA_ROPE
MLA_QK = 2 * LANES
VX = 2 * LANES
LOG2E = math.log2(math.e)

BF16 = jnp.bfloat16
F32 = jnp.float32
VMEM_LIMIT_MB = 56


def _rms(x, g):
    return x * lax.rsqrt(jnp.mean(x * x, axis=-1, keepdims=True) + EPS) * g


def _dot(a, b):
    return jnp.dot(a, b, preferred_element_type=F32)


def _dot_nt(a, b):
    return lax.dot_general(a, b, (((1,), (1,)), ((), ())), preferred_element_type=F32)


def _params(n_axes):
    return pltpu.CompilerParams(dimension_semantics=("arbitrary",) * n_axes,
                                vmem_limit_bytes=VMEM_LIMIT_MB * 2 ** 20)


def _full(shape):
    zeros = (0,) * len(shape)
    return pl.BlockSpec(shape, lambda *_: zeros)


def _ffn_kernel(h_ref, g_ref, wu_ref, wd_ref, gf_ref, o_ref, xn_ref, *, final_norm):
    f = pl.program_id(1)

    @pl.when(f == 0)
    def _():
        h = h_ref[...]
        xn_ref[...] = _rms(h, g_ref[...]).astype(BF16)
        o_ref[...] = h

    a = jnp.maximum(_dot(xn_ref[...], wu_ref[0]), 0.0)
    o_ref[...] += _dot((a * a).astype(BF16), wd_ref[0])

    if final_norm:
        @pl.when(f == pl.num_programs(1) - 1)
        def _():
            o_ref[...] = _rms(o_ref[...], gf_ref[...])


def _ffn(h, g, wu, wd, li, gf, final_norm):
    m = h.shape[0]
    tm, tf = min(m, 512), 1024
    return pl.pallas_call(
        functools.partial(_ffn_kernel, final_norm=final_norm),
        grid=(m // tm, D_FF // tf),
        in_specs=[pl.BlockSpec((tm, D_MODEL), lambda i, f: (i, 0)),
                  _full((1, D_MODEL)),
                  pl.BlockSpec((1, D_MODEL, tf), lambda i, f: (li, 0, f)),
                  pl.BlockSpec((1, tf, D_MODEL), lambda i, f: (li, f, 0)),
                  _full((1, D_MODEL))],
        out_specs=pl.BlockSpec((tm, D_MODEL), lambda i, f: (i, 0)),
        out_shape=jax.ShapeDtypeStruct((m, D_MODEL), F32),
        scratch_shapes=[pltpu.VMEM((tm, D_MODEL), BF16)],
        compiler_params=_params(2), name="ffn")(h, g, wu, wd, gf)


def _mm_res_kernel(*refs, n):
    acc = refs[2 * n][...]
    for y_ref, w_ref in zip(refs[:n], refs[n:2 * n]):
        acc = acc + _dot(y_ref[...], w_ref[...])
    refs[2 * n + 1][...] = acc


def _mm_res(ys, ws, res):
    m = res.shape[0]
    tm = min(m, 512)
    n = len(ys)
    return pl.pallas_call(
        functools.partial(_mm_res_kernel, n=n),
        grid=(m // tm,),
        in_specs=([pl.BlockSpec((tm, y.shape[1]), lambda i: (i, 0)) for y in ys]
                  + [_full(w.shape) for w in ws]
                  + [pl.BlockSpec((tm, D_MODEL), lambda i: (i, 0))]),
        out_specs=pl.BlockSpec((tm, D_MODEL), lambda i: (i, 0)),
        out_shape=jax.ShapeDtypeStruct((m, D_MODEL), F32),
        compiler_params=_params(1), name="mm_res")(*ys, *ws, res)


def _ab_in_kernel(h_ref, g_ref, w_ref, sgn_ref, o_ref, xn_ref):
    j = pl.program_id(1)

    @pl.when(j == 0)
    def _():
        xn_ref[...] = _rms(h_ref[...], g_ref[...]).astype(BF16)

    p = _dot(xn_ref[...], w_ref[...])

    @pl.when(j == 0)
    def _():
        o_ref[...] = p

    @pl.when(j == 1)
    def _():
        o_ref[...] = jax.nn.gelu(p)

    @pl.when(j == 2)
    def _():
        o_ref[...] = _rms(jax.nn.gelu(p), sgn_ref[...])


def _ab_in(h, g, w, sgn):
    m = h.shape[0]
    tm = min(m, 512)
    return pl.pallas_call(
        _ab_in_kernel,
        grid=(m // tm, 3),
        in_specs=[pl.BlockSpec((tm, D_MODEL), lambda i, j: (i, 0)),
                  _full((1, D_MODEL)),
                  pl.BlockSpec((D_MODEL, POOL_WIDTH), lambda i, j: (0, j)),
                  _full((1, SG_WIDTH))],
        out_specs=pl.BlockSpec((tm, POOL_WIDTH), lambda i, j: (i, j)),
        out_shape=jax.ShapeDtypeStruct((m, 3 * POOL_WIDTH), F32),
        scratch_shapes=[pltpu.VMEM((tm, D_MODEL), BF16)],
        compiler_params=_params(2), name="ab_in")(h, g, w, sgn)


def _ab_mix_kernel(z_ref, u_ref, v_ref, pw_ref, ps_ref, sw_ref, sbt_ref, o_ref, zbuf_ref, *, tm):
    t = pl.program_id(1)
    carry = 16

    @pl.when(t == 0)
    def _():
        zbuf_ref[0:carry, :] = jnp.zeros((carry, POOL_WIDTH), F32)

    z = z_ref[...]
    zbuf_ref[carry:carry + tm, :] = z
    pos = t * tm + lax.broadcasted_iota(jnp.int32, (tm, 1), 0)
    for gi, w in enumerate(POOL_WINDOWS):
        lo, hi = gi * POOL_GW, (gi + 1) * POOL_GW
        s = z[:, lo:hi]
        for i in range(1, w):
            s = s + zbuf_ref[pl.ds(carry - i, tm), lo:hi]
        cnt = jnp.minimum(pos + 1, w).astype(F32)
        pooled = s / cnt - z[:, lo:hi]
        ya = _dot(pooled.astype(BF16), pw_ref[gi]) * ps_ref[:, lo:hi]
        o_ref[:, lo:hi] = ya.astype(BF16)
    zbuf_ref[0:carry, :] = zbuf_ref[tm:tm + carry, :]

    row = lax.broadcasted_iota(jnp.int32, (SG_CHUNK, SG_CHUNK), 0)
    col = lax.broadcasted_iota(jnp.int32, (SG_CHUNK, SG_CHUNK), 1)
    for g in range(SG_GROUPS):
        wg = jnp.where(row >= col, sw_ref[g], 0.0).astype(BF16)
        bg = sbt_ref[:, g:g + 1]
        lo, hi = g * SG_GW, (g + 1) * SG_GW
        for c in range(tm // SG_CHUNK):
            r0, r1 = c * SG_CHUNK, (c + 1) * SG_CHUNK
            mixed = _dot(wg, v_ref[r0:r1, lo:hi].astype(BF16)) + bg
            o_ref[r0:r1, POOL_WIDTH + lo:POOL_WIDTH + hi] = (u_ref[r0:r1, lo:hi] * mixed).astype(BF16)


def _ab_mix(zuv, batch, seq, pool_w, pool_scale, sg_w, sg_bt):
    tm = min(seq, 512)
    nt = seq // tm
    col = lambda c: pl.BlockSpec((tm, POOL_WIDTH), lambda b, t: (b * nt + t, c))
    return pl.pallas_call(
        functools.partial(_ab_mix_kernel, tm=tm),
        grid=(batch, nt),
        in_specs=[col(0), col(1), col(2),
                  _full(pool_w.shape), _full(pool_scale.shape), _full(sg_w.shape), _full(sg_bt.shape)],
        out_specs=pl.BlockSpec((tm, D_MODEL), lambda b, t: (b * nt + t, 0)),
        out_shape=jax.ShapeDtypeStruct((batch * seq, D_MODEL), BF16),
        scratch_shapes=[pltpu.VMEM((tm + 16, POOL_WIDTH), F32)],
        compiler_params=_params(2), name="ab_mix")(zuv, zuv, zuv, pool_w, pool_scale, sg_w, sg_bt)


def _ab_mix_s_kernel(zuv_ref, st_ref, pw_ref, ps_ref, w0_ref, b0_ref, o_ref, *, cnts):
    z = zuv_ref[:, 0:POOL_WIDTH]
    for gi, w in enumerate(POOL_WINDOWS):
        lo, hi = gi * POOL_GW, (gi + 1) * POOL_GW
        s = z[:, lo:hi]
        for i in range(1, w):
            s = s + st_ref[POOL_BUF - i, :, lo:hi]
        pooled = s / cnts[gi] - z[:, lo:hi]
        ya = _dot(pooled.astype(BF16), pw_ref[gi]) * ps_ref[:, lo:hi]
        o_ref[:, lo:hi] = ya.astype(BF16)
    u = zuv_ref[:, POOL_WIDTH:POOL_WIDTH + SG_WIDTH]
    v = zuv_ref[:, POOL_WIDTH + SG_WIDTH:]
    o_ref[:, POOL_WIDTH:] = (u * (w0_ref[...] * v + b0_ref[...])).astype(BF16)


def _ab_mix_s(zuv, state, pool_w, pool_scale, w0, b0, cnts):
    m = zuv.shape[0]
    return pl.pallas_call(
        functools.partial(_ab_mix_s_kernel, cnts=cnts),
        grid=(1,),
        in_specs=[_full(zuv.shape), _full(state.shape), _full(pool_w.shape), _full(pool_scale.shape),
                  _full(w0.shape), _full(b0.shape)],
        out_specs=_full((m, D_MODEL)),
        out_shape=jax.ShapeDtypeStruct((m, D_MODEL), BF16),
        compiler_params=_params(1), name="ab_mix_s")(zuv, state, pool_w, pool_scale, w0, b0)


def _rope128(t, cs):
    u = t * cs
    return u + pltpu.roll(u, 2 * 32, axis=1)


def _cd_in_kernel(h_ref, g_ref, win_ref, qn_ref, wqb_ref, kvn_ref, wkv_ref, cs_ref, *outs, sample):
    xn = _rms(h_ref[...], g_ref[...]).astype(BF16)
    p = _dot(xn, win_ref[...])
    tm = p.shape[0]
    lane = lax.broadcasted_iota(jnp.int32, (tm, LANES), 1)
    first = lane < MLA_ROPE
    cs = cs_ref[...]
    o_k, o_v = CD_Q, CD_Q + CD_K
    o_cq, o_ckv, o_pe = o_v + CD_V, o_v + CD_V + MLA_Q_LORA, o_v + CD_V + MLA_Q_LORA + MLA_KV_LORA
    dk, dv = p[:, o_k:o_v], p[:, o_v:o_cq]
    cqn = _rms(p[:, o_cq:o_ckv], qn_ref[...]).astype(BF16)
    ckv = _rms(p[:, o_ckv:o_pe], kvn_ref[...])
    rk = _rope128(p[:, o_pe:o_pe + LANES], cs)
    q = _dot(cqn, wqb_ref[...])
    q_nope = MLA_HEADS * MLA_NOPE
    if sample:
        qd_ref, dk_ref, dv_ref, ckv_ref, kpe_ref, qlat_ref, qpe_ref = outs
        qd_ref[...] = p[:, 0:CD_Q] * DIFF_SCALE
        for h in range(MLA_HEADS):
            qn = q[:, h * MLA_NOPE:(h + 1) * MLA_NOPE].astype(BF16)
            qlat_ref[:, h * MLA_KV_LORA:(h + 1) * MLA_KV_LORA] = _dot(qn, wkv_ref[h]).astype(BF16)
            qpe_ref[:, h * LANES:(h + 1) * LANES] = _rope128(
                q[:, q_nope + h * LANES:q_nope + (h + 1) * LANES], cs)
    else:
        qd_ref, dk_ref, dv_ref, dkb_ref, dvx_ref, ckv_ref, kpe_ref, qm_ref, km_ref, vmx_ref = outs
        ones = jnp.ones((tm, LANES), BF16)
        for hg in range(DIFF_HEADS):
            slot = p[:, hg * LANES:(hg + 1) * LANES] * (DIFF_SCALE * LOG2E)
            h, g = divmod(hg, DIFF_GROUP)
            qd_ref[0, (h * 2 + 0) * DIFF_GROUP + g] = jnp.where(first, slot, 0.0).astype(BF16)
            qd_ref[0, (h * 2 + 1) * DIFF_GROUP + g] = jnp.where(first, 0.0, slot).astype(BF16)
        dkb_ref[...] = dk.astype(BF16)
        for h in range(DIFF_KV_HEADS):
            dvx_ref[:, h * VX:h * VX + DIFF_VD] = dv[:, h * DIFF_VD:(h + 1) * DIFF_VD].astype(BF16)
            dvx_ref[:, h * VX + DIFF_VD:(h + 1) * VX] = ones
        kv = _dot(ckv.astype(BF16), wkv_ref[...])
        kpe_slot = jnp.where(first, rk, 0.0).astype(BF16)
        q = q * (MLA_SCALE * LOG2E)
        for h in range(MLA_HEADS):
            qm_ref[:, h * MLA_QK:h * MLA_QK + LANES] = q[:, h * MLA_NOPE:(h + 1) * MLA_NOPE].astype(BF16)
            qm_ref[:, h * MLA_QK + LANES:(h + 1) * MLA_QK] = _rope128(
                q[:, q_nope + h * LANES:q_nope + (h + 1) * LANES], cs).astype(BF16)
            km_ref[:, h * MLA_QK:h * MLA_QK + LANES] = kv[:, h * 2 * LANES:h * 2 * LANES + LANES].astype(BF16)
            km_ref[:, h * MLA_QK + LANES:(h + 1) * MLA_QK] = kpe_slot
            vmx_ref[:, h * VX:h * VX + MLA_VD] = kv[:, h * 2 * LANES + LANES:(h + 1) * 2 * LANES].astype(BF16)
            vmx_ref[:, h * VX + MLA_VD:(h + 1) * VX] = ones
    dk_ref[...] = dk
    dv_ref[...] = dv
    ckv_ref[...] = ckv
    kpe_ref[...] = rk[:, 0:MLA_ROPE]


def _cd_in(h, g, win, qn, wqb, kvn, wkv, cs, batch, seq, sample):
    m = h.shape[0]
    tm = min(seq, 256) if not sample else m
    nt = seq // tm if not sample else 1
    row = lambda width, dt: (pl.BlockSpec((tm, width), lambda i: (i, 0)), jax.ShapeDtypeStruct((m, width), dt))
    common = [row(CD_K, F32), row(CD_V, F32)]
    tail = [row(MLA_KV_LORA, F32), row(MLA_ROPE, F32)]
    if sample:
        outs = ([row(CD_Q, F32)] + common + tail
                + [row(MLA_HEADS * MLA_KV_LORA, BF16), row(MLA_HEADS * LANES, F32)])
        cs_spec = _full(cs.shape)
    else:
        qd = (pl.BlockSpec((1, DIFF_MAPS, tm, LANES), lambda i: (i // nt, 0, i % nt, 0)),
              jax.ShapeDtypeStruct((batch, DIFF_MAPS, seq, LANES), BF16))
        outs = ([qd] + common + [row(CD_K, BF16), row(DIFF_KV_HEADS * VX, BF16)] + tail
                + [row(MLA_HEADS * MLA_QK, BF16), row(MLA_HEADS * MLA_QK, BF16), row(MLA_HEADS * VX, BF16)])
        cs_spec = pl.BlockSpec((tm, LANES), lambda i: (i % nt, 0))
    return pl.pallas_call(
        functools.partial(_cd_in_kernel, sample=sample),
        grid=(m // tm,),
        in_specs=[pl.BlockSpec((tm, D_MODEL), lambda i: (i, 0)), _full(g.shape), _full(win.shape),
                  _full(qn.shape), _full(wqb.shape), _full(kvn.shape), _full(wkv.shape), cs_spec],
        out_specs=[o[0] for o in outs],
        out_shape=[o[1] for o in outs],
        compiler_params=_params(1), name="cd_in_s" if sample else "cd_in")(h, g, win, qn, wqb, kvn, wkv, cs)


def _bias_kernel(tab_ref, dist_ref, bn_ref, bs_ref, on_ref, os_ref):
    def lookup(bkt, head):
        val = jnp.zeros(bkt.shape, F32)
        for b in range(REL_BUCKETS):
            val = jnp.where(bkt == b, tab_ref[b, head], val)
        return val

    future = dist_ref[...] < 0
    bn, bs = bn_ref[...], bs_ref[...]
    for head in range(DIFF_HEADS):
        far = tab_ref[REL_BUCKETS - 1, head]
        on_ref[head] = jnp.where(future, NEG_INF, (lookup(bn, head) - far) * LOG2E)
        os_ref[head] = lookup(bs, head)


def _bias_tiles(rel_table, dist_near, bkt_near, bkt_sample):
    vmem = pl.BlockSpec(memory_space=pltpu.VMEM)
    return pl.pallas_call(
        _bias_kernel,
        in_specs=[pl.BlockSpec(memory_space=pltpu.SMEM), vmem, vmem, vmem],
        out_specs=[vmem, vmem],
        out_shape=[jax.ShapeDtypeStruct((DIFF_HEADS,) + bkt_near.shape, F32),
                   jax.ShapeDtypeStruct((DIFF_HEADS,) + bkt_sample.shape, F32)],
        name="rel_bias")(rel_table, dist_near, bkt_near, bkt_sample)


def _attn_block(q, k, vx, m_ref, acc_ref, row0, near=None):
    rb, c = q.shape[0], k.shape[0]
    s = _dot_nt(q, k)
    if near is not None:
        s = near(s)
    rows = pl.ds(row0, rb)
    m_prev = m_ref[rows, :]
    m_new = jnp.maximum(m_prev, jnp.max(s, axis=-1, keepdims=True))
    alpha = jnp.exp2(m_prev - m_new)
    p = jnp.exp2(s - jnp.concatenate([m_new] * (c // LANES), axis=1)).astype(BF16)
    acc_ref[rows, :] = jnp.concatenate([alpha] * (VX // LANES), axis=1) * acc_ref[rows, :] + _dot(p, vx)
    m_ref[rows, :] = m_new


def _attn_init(m_ref, acc_ref):
    m_ref[...] = jnp.full(m_ref.shape, NEG_INF, F32)
    acc_ref[...] = jnp.zeros(acc_ref.shape, F32)


def _causal_pairs(n):
    pairs = [(i, j) for i in range(n) for j in range(i + 1)]
    return (jnp.asarray([p[0] for p in pairs], jnp.int32), jnp.asarray([p[1] for p in pairs], jnp.int32))


def _softmax_init(m_ref, l_ref, acc_ref):
    m_ref[...] = jnp.full(m_ref.shape, NEG_INF, F32)
    l_ref[...] = jnp.zeros(l_ref.shape, F32)
    acc_ref[...] = jnp.zeros(acc_ref.shape, F32)


def _diff_lambda(lp, li):
    lam_init = 0.8 - 0.6 * math.exp(-0.3 * li)
    a = jnp.sum(lp[0:1, :] * lp[1:2, :], axis=-1, keepdims=True)
    b = jnp.sum(lp[2:3, :] * lp[3:4, :], axis=-1, keepdims=True)
    return jnp.exp(a) - jnp.exp(b) + lam_init, lam_init


def _diff_attn_kernel(ii_ref, jj_ref, q_ref, k_ref, vx_ref, near_ref, lam_ref, sub_ref, o_ref,
                      m_ref, acc_ref, *, tb, rb, li):
    t = pl.program_id(2)
    i, j = ii_ref[t], jj_ref[t]
    slots = 2 * DIFF_GROUP
    nrb = tb // rb
    band = REL_MAX_DIST

    @pl.when(j == 0)
    def _():
        _attn_init(m_ref, acc_ref)

    def tile(kind):
        k, vx = k_ref[...], vx_ref[...]
        for slot in range(slots):
            g = slot % DIFF_GROUP
            for b in range(nrb):
                r0 = b * rb
                q = q_ref[0, slot, r0:r0 + rb, :]
                kk, vv, near = k, vx, None
                if kind == "prev" and b == 0:
                    near = lambda s, g=g: jnp.concatenate(
                        [s[:, :tb - band], s[:, tb - band:] + near_ref[g, :, 0:band]], axis=1)
                elif kind == "diag":
                    kk, vv = k[:r0 + rb], vx[:r0 + rb]
                    if b == 0:
                        near = lambda s, g=g: s + near_ref[g, :, band:]
                    elif r0 == band:
                        near = lambda s, g=g: s + near_ref[g]
                    else:
                        near = lambda s, g=g, r0=r0: jnp.concatenate(
                            [s[:, :r0 - band], s[:, r0 - band:] + near_ref[g]], axis=1)
                _attn_block(q, kk, vv, m_ref, acc_ref, slot * tb + r0, near)

    @pl.when(j < i - 1)
    def _():
        tile("far")

    @pl.when(j == i - 1)
    def _():
        tile("prev")

    @pl.when(j == i)
    def _():
        tile("diag")
        lam, lam_init = _diff_lambda(lam_ref[...], li)
        for g in range(DIFF_GROUP):
            a0 = acc_ref[g * tb:(g + 1) * tb, :]
            a1 = acc_ref[(DIFF_GROUP + g) * tb:(DIFF_GROUP + g + 1) * tb, :]
            y = a0[:, :DIFF_VD] / a0[:, DIFF_VD:] - lam * (a1[:, :DIFF_VD] / a1[:, DIFF_VD:])
            o_ref[:, g * DIFF_VD:(g + 1) * DIFF_VD] = (_rms(y, sub_ref[...]) * (1.0 - lam_init)).astype(BF16)


def _diff_attn(qd, kb, vx, near, lam_p, subln, batch, seq, tb, rb, li):
    n = seq // tb
    slots = 2 * DIFF_GROUP
    ii, jj = _causal_pairs(n)
    grid_spec = pltpu.PrefetchScalarGridSpec(
        num_scalar_prefetch=2, grid=(batch, DIFF_KV_HEADS, ii.shape[0]),
        in_specs=[pl.BlockSpec((1, slots, tb, LANES), lambda b, h, t, ii, jj: (b, h, ii[t], 0)),
                  pl.BlockSpec((tb, LANES), lambda b, h, t, ii, jj: (b * n + jj[t], h)),
                  pl.BlockSpec((tb, VX), lambda b, h, t, ii, jj: (b * n + jj[t], h)),
                  pl.BlockSpec((DIFF_GROUP,) + near.shape[1:], lambda b, h, t, ii, jj: (h, 0, 0)),
                  pl.BlockSpec(lam_p.shape, lambda b, h, t, ii, jj: (0, 0)),
                  pl.BlockSpec(subln.shape, lambda b, h, t, ii, jj: (0, 0))],
        out_specs=pl.BlockSpec((tb, DIFF_GROUP * DIFF_VD), lambda b, h, t, ii, jj: (b * n + ii[t], h)),
        scratch_shapes=[pltpu.VMEM((slots * tb, LANES), F32), pltpu.VMEM((slots * tb, VX), F32)])
    return pl.pallas_call(
        functools.partial(_diff_attn_kernel, tb=tb, rb=rb, li=li),
        grid_spec=grid_spec,
        out_shape=jax.ShapeDtypeStruct((batch * seq, DIFF_HEADS * DIFF_VD), BF16),
        compiler_params=_params(3), name="diff_attn")(ii, jj, qd, kb, vx, near, lam_p, subln)


def _mla_attn_kernel(ii_ref, jj_ref, q_ref, k_ref, vx_ref, o_ref, m_ref, acc_ref, *, tb, rb):
    t = pl.program_id(2)
    i, j = ii_ref[t], jj_ref[t]

    @pl.when(j == 0)
    def _():
        _attn_init(m_ref, acc_ref)

    def tile(diag):
        k, vx = k_ref[...], vx_ref[...]
        causal = (lax.broadcasted_iota(jnp.int32, (rb, rb), 0) >= lax.broadcasted_iota(jnp.int32, (rb, rb), 1))
        for b in range(tb // rb):
            r0 = b * rb
            q = q_ref[r0:r0 + rb, :]
            if not diag:
                _attn_block(q, k, vx, m_ref, acc_ref, r0)
            elif b == 0:
                _attn_block(q, k[:rb], vx[:rb], m_ref, acc_ref, r0, lambda s: jnp.where(causal, s, NEG_INF))
            else:
                near = lambda s, r0=r0: jnp.concatenate(
                    [s[:, :r0], jnp.where(causal, s[:, r0:], NEG_INF)], axis=1)
                _attn_block(q, k[:r0 + rb], vx[:r0 + rb], m_ref, acc_ref, r0, near)

    @pl.when(j < i)
    def _():
        tile(False)

    @pl.when(j == i)
    def _():
        tile(True)
        acc = acc_ref[...]
        o_ref[...] = (acc[:, :MLA_VD] / acc[:, MLA_VD:]).astype(BF16)


def _mla_attn(qm, km, vmx, batch, seq, tb, rb):
    n = seq // tb
    ii, jj = _causal_pairs(n)
    grid_spec = pltpu.PrefetchScalarGridSpec(
        num_scalar_prefetch=2, grid=(batch, MLA_HEADS, ii.shape[0]),
        in_specs=[pl.BlockSpec((tb, MLA_QK), lambda b, h, t, ii, jj: (b * n + ii[t], h)),
                  pl.BlockSpec((tb, MLA_QK), lambda b, h, t, ii, jj: (b * n + jj[t], h)),
                  pl.BlockSpec((tb, VX), lambda b, h, t, ii, jj: (b * n + jj[t], h))],
        out_specs=pl.BlockSpec((tb, MLA_VD), lambda b, h, t, ii, jj: (b * n + ii[t], h)),
        scratch_shapes=[pltpu.VMEM((tb, LANES), F32), pltpu.VMEM((tb, VX), F32)])
    return pl.pallas_call(
        functools.partial(_mla_attn_kernel, tb=tb, rb=rb),
        grid_spec=grid_spec,
        out_shape=jax.ShapeDtypeStruct((batch * seq, MLA_HEADS * MLA_VD), BF16),
        compiler_params=_params(3), name="mla_attn")(ii, jj, qm, km, vmx)


def _decode_kernel(pt_ref, qd_ref, ql_ref, qp_ref, bias_ref, bnew_ref, kdn_ref, vdn_ref, ckvn_ref, kpen_ref,
                   cdk_ref, cdv_ref, cckv_ref, ckpe_ref, od_ref, om_ref,
                   kd0, vd0, ckv0, kpe0, kd1, vd1, ckv1, kpe1, sem,
                   md_ref, ld_ref, accd_ref, mm_ref, lm_ref, accm_ref, *, pages, nch):
    s = pl.program_id(0)
    last = pl.num_programs(0) - 1
    c = lax.rem(s, nch)
    bufs = ((kd0, vd0, ckv0, kpe0), (kd1, vd1, ckv1, kpe1))
    drows = PAGE_SIZE * DIFF_KV_HEADS

    def page_copies(chunk, slot):
        out = []
        for k in range(pages):
            pg = 0 if chunk is None else pt_ref[chunk * pages + k]
            row = pl.multiple_of(pg * drows, drows)
            srcs = (cdk_ref.at[pl.ds(row, drows), :], cdv_ref.at[pl.ds(row, drows), :],
                    cckv_ref.at[pg], ckpe_ref.at[pg])
            for a, src in enumerate(srcs):
                out.append(pltpu.make_async_copy(src, bufs[slot][a].at[k], sem.at[slot, a]))
        return out

    def start(chunk, slot):
        for cp in page_copies(chunk, slot):
            cp.start()

    def wait(slot):
        for cp in page_copies(None, slot):
            cp.wait()

    @pl.when(s == 0)
    def _():
        start(0, 0)

    @pl.when(c == 0)
    def _():
        _softmax_init(md_ref, ld_ref, accd_ref)
        _softmax_init(mm_ref, lm_ref, accm_ref)

    qd, ql, qp = qd_ref[0], ql_ref[0], qp_ref[0]

    def update(s, vals, m_ref, l_ref, acc_ref):
        m_prev = m_ref[...]
        m_new = jnp.maximum(m_prev, jnp.max(s, axis=-1, keepdims=True))
        alpha = jnp.exp(m_prev - m_new)
        p = jnp.exp(s - m_new)
        l_ref[...] = alpha * l_ref[...] + jnp.sum(p, axis=-1, keepdims=True)
        pb = p.astype(BF16)
        acc = alpha * acc_ref[...]
        width = vals[0].shape[0]
        for k, val in enumerate(vals):
            acc = acc + _dot(pb[:, k * width:(k + 1) * width], val)
        acc_ref[...] = acc
        m_ref[...] = m_new

    def attend(slot):
        kd, vd, ckv, kpe = bufs[slot]
        vds = [vd[k].astype(BF16) for k in range(pages)]
        ckvs = [ckv[k].astype(BF16) for k in range(pages)]
        sd = (jnp.concatenate([_dot_nt(qd, kd[k].astype(BF16)) for k in range(pages)], axis=1)
              + bias_ref[:, slot * pages * drows:(slot + 1) * pages * drows])
        sm = jnp.concatenate([_dot_nt(ql, ckvs[k]) + _dot(qp, kpe[k].astype(BF16))
                              for k in range(pages)], axis=1) * MLA_SCALE
        update(sd, vds, md_ref, ld_ref, accd_ref)
        update(sm, ckvs, mm_ref, lm_ref, accm_ref)

    wait(0)
    start(2 * s + 1, 1)
    attend(0)
    wait(1)
    start(jnp.where(s == last, 0, 2 * s + 2), 0)
    attend(1)

    @pl.when(s == last)
    def _():
        wait(0)

    @pl.when(c == nch - 1)
    def _():
        def finish(s_new, v_new, m_ref, l_ref, acc_ref, o_ref):
            m_prev = m_ref[...]
            m_new = jnp.maximum(m_prev, s_new)
            alpha = jnp.exp(m_prev - m_new)
            p = jnp.exp(s_new - m_new)
            l = alpha * l_ref[...] + p
            o_ref[0] = (alpha * acc_ref[...] + p * v_new) / l

        head0 = lax.broadcasted_iota(jnp.int32, (DIFF_MAPS, LANES), 0) < 2 * DIFF_GROUP
        kdn, vdn = kdn_ref[0], vdn_ref[0]
        kd_new = jnp.where(head0, kdn[:, 0:LANES], kdn[:, LANES:])
        vd_new = jnp.where(head0, vdn[:, 0:DIFF_VD], vdn[:, DIFF_VD:])
        sd_new = jnp.sum(qd.astype(F32) * kd_new, axis=-1, keepdims=True) + bnew_ref[:, 0:1]
        finish(sd_new, vd_new, md_ref, ld_ref, accd_ref, od_ref)
        sm_new = (jnp.sum(ql.astype(F32) * ckvn_ref[0], axis=-1, keepdims=True)
                  + jnp.sum(qp.astype(F32) * kpen_ref[0], axis=-1, keepdims=True)) * MLA_SCALE
        finish(sm_new, ckvn_ref[0], mm_ref, lm_ref, accm_ref, om_ref)


def _decode(page_idx, qd, ql, qp, bias, bias_new, kd_new, vd_new, ckv_new, kpe_new, c_dk, c_dv, c_ckv, c_kpe,
            n_pages):
    nb = qd.shape[0]
    pages = next(p for p in (8, 4, 2, 1) if n_pages % (2 * p) == 0)
    nch = n_pages // (2 * pages)
    rows = DIFF_MAPS
    drows = PAGE_SIZE * DIFF_KV_HEADS
    per_b = lambda shape: pl.BlockSpec((1,) + shape, lambda s, pt: (s // nch, 0, 0))
    hbm = pl.BlockSpec(memory_space=pl.ANY)
    page_bufs = [pltpu.VMEM((pages, drows, LANES), F32), pltpu.VMEM((pages, drows, LANES), F32),
                 pltpu.VMEM((pages, PAGE_SIZE, MLA_KV_LORA), F32), pltpu.VMEM((pages, MLA_ROPE, PAGE_SIZE), F32)]
    in_specs = [per_b((rows, LANES)), per_b((rows, MLA_KV_LORA)), per_b((rows, MLA_ROPE)),
                pl.BlockSpec((rows, 2 * pages * drows), lambda s, pt: (0, s % nch)),
                pl.BlockSpec(bias_new.shape, lambda s, pt: (0, 0)),
                per_b((1, CD_K)), per_b((1, CD_V)), per_b((1, MLA_KV_LORA)), per_b((1, MLA_ROPE)),
                hbm, hbm, hbm, hbm]
    grid_spec = pltpu.PrefetchScalarGridSpec(
        num_scalar_prefetch=1, grid=(nb * nch,), in_specs=in_specs,
        out_specs=[per_b((rows, DIFF_VD)), per_b((rows, MLA_KV_LORA))],
        scratch_shapes=(page_bufs + page_bufs + [pltpu.SemaphoreType.DMA((2, 4))]
                        + [pltpu.VMEM((rows, 1), F32), pltpu.VMEM((rows, 1), F32), pltpu.VMEM((rows, DIFF_VD), F32),
                           pltpu.VMEM((rows, 1), F32), pltpu.VMEM((rows, 1), F32),
                           pltpu.VMEM((rows, MLA_KV_LORA), F32)]))
    return pl.pallas_call(
        functools.partial(_decode_kernel, pages=pages, nch=nch),
        grid_spec=grid_spec,
        out_shape=[jax.ShapeDtypeStruct((nb, rows, DIFF_VD), F32),
                   jax.ShapeDtypeStruct((nb, rows, MLA_KV_LORA), F32)],
        compiler_params=_params(1), name="decode_attn")(
            page_idx, qd, ql, qp, bias, bias_new, kd_new, vd_new, ckv_new, kpe_new, c_dk, c_dv, c_ckv, c_kpe)


def _sample_out_kernel(o0_ref, o1_ref, om_ref, lam_ref, sub_ref, wuv_ref, w1_ref, w2_ref, res_ref, o_ref, *, li):
    lam, lam_init = _diff_lambda(lam_ref[...], li)
    acc = res_ref[...]
    for hd in range(DIFF_HEADS):
        lo, hi = hd * DIFF_VD, (hd + 1) * DIFF_VD
        y = o0_ref[:, lo:hi] - lam * o1_ref[:, lo:hi]
        yd = (_rms(y, sub_ref[...]) * (1.0 - lam_init)).astype(BF16)
        acc = acc + _dot(yd, w1_ref[lo:hi, :])
    for h in range(MLA_HEADS):
        om = om_ref[:, h * MLA_KV_LORA:(h + 1) * MLA_KV_LORA].astype(BF16)
        ym = _dot(om, wuv_ref[h]).astype(BF16)
        acc = acc + _dot(ym, w2_ref[h * MLA_VD:(h + 1) * MLA_VD, :])
    o_ref[...] = acc


def _sample_out(o0, o1, om, lam_p, subln, wuv, w1, w2, res, li):
    args = (o0, o1, om, lam_p, subln, wuv, w1, w2, res)
    return pl.pallas_call(
        functools.partial(_sample_out_kernel, li=li),
        grid=(1,),
        in_specs=[_full(a.shape) for a in args],
        out_specs=_full(res.shape),
        out_shape=jax.ShapeDtypeStruct(res.shape, F32),
        compiler_params=_params(1), name="sample_out")(*args)


def _rope_table(pos):
    half = MLA_ROPE // 2
    freq = ROPE_BASE ** (-jnp.arange(half, dtype=F32) / half)
    ang = pos.astype(F32)[:, None] * freq[None, :]
    c, s = jnp.cos(ang), jnp.sin(ang)
    return jnp.concatenate([c, c, -s, s], axis=-1)


def _rel_bucket(n):
    exact = REL_BUCKETS // 2
    nf = jnp.maximum(n, 1).astype(F32)
    large = exact + (jnp.log(nf / exact) / math.log(REL_MAX_DIST / exact)
                     * (REL_BUCKETS - exact)).astype(jnp.int32)
    return jnp.where(n < exact, n, jnp.minimum(large, REL_BUCKETS - 1))


def _row2(x):
    return x.reshape(1, -1)


def _diff_tiles(seq):
    tb = min(seq, 512)
    return tb, min(tb, 256)


def _mla_tiles(seq):
    tb = min(seq, 1024)
    return tb, min(tb, 256)


def kernel(x_prompt, x_sample, state_pool, cache_diff_k, cache_diff_v, cache_mla_ckv, cache_mla_kpe, page_table, norm_mix, norm_ffn, norm_final, w_ffn_up, w_ffn_down, w_in_ab, pool_w, pool_scale, sg_norm, sg_w, sg_b, w_out_ab, w_in_cd, diff_lambda, diff_subln, mla_q_norm, mla_w_qb, mla_kv_norm, mla_w_kvb, w_out_cd, rel_table):
    batch, seq, _ = x_prompt.shape
    nb, dec_seq, _ = x_sample.shape
    assert dec_seq == 1
    n_pages = page_table.shape[1]
    past_len = n_pages * PAGE_SIZE
    depth = norm_mix.shape[0]
    n_odd = cache_diff_k.shape[1]
    hp = x_prompt.reshape(batch * seq, D_MODEL)
    hs = x_sample.reshape(nb, D_MODEL)
    gf = _row2(norm_final)
    wu_all, wd_all = w_ffn_up.astype(BF16), w_ffn_down.astype(BF16)

    pool_p, pool_s, sgv_s, rows_p, rows_s = [], [], [], [], []
    for li in range(depth):
        e = li // 2
        g_mix, g_ffn = _row2(norm_mix[li]), _row2(norm_ffn[li])
        if li % 2 == 0:
            w_in = w_in_ab[e].astype(BF16)
            pw = pool_w[e].astype(BF16)
            ps = _row2(pool_scale[e])
            sgn = _row2(sg_norm[e])
            w_out = w_out_ab[e].astype(BF16)
            zuv = _ab_in(hp, g_mix, w_in, sgn)
            yab = _ab_mix(zuv, batch, seq, pw, ps, sg_w[e], sg_b[e, :, :SG_CHUNK].T)
            pool_p.append(zuv.reshape(batch, seq, -1)[:, seq - POOL_BUF:, :POOL_WIDTH])
            hp = _mm_res([yab], [w_out], hp)
            zuv_s = _ab_in(hs, g_mix, w_in, sgn)
            w0 = _row2(jnp.repeat(sg_w[e, :, 0, 0], SG_GW))
            b0 = _row2(jnp.repeat(sg_b[e, :, 0], SG_GW))
            cnts = tuple(float(min(past_len + 1, w)) for w in POOL_WINDOWS)
            yab_s = _ab_mix_s(zuv_s, jnp.transpose(state_pool[e], (1, 0, 2)), pw, ps, w0, b0, cnts)
            pool_s.append(jnp.concatenate([state_pool[e][:, 1:], zuv_s[:, None, :POOL_WIDTH]], axis=1))
            sgv_s.append(zuv_s[:, None, POOL_WIDTH + SG_WIDTH:])
            hs = _mm_res([yab_s], [w_out], hs)
        else:
            w = w_in_cd[e]
            wq = w[:, :CD_Q].reshape(D_MODEL, DIFF_KV_HEADS, 2, DIFF_GROUP, DIFF_DH)
            wq = jnp.transpose(wq, (0, 1, 3, 2, 4)).reshape(D_MODEL, CD_Q)
            half = MLA_ROPE // 2
            w_pe = w[:, -MLA_ROPE:]
            w_in = jnp.concatenate([wq, w[:, CD_Q:], w_pe[:, half:], w_pe[:, :half]], axis=1).astype(BF16)
            wqb = mla_w_qb[e].reshape(MLA_Q_LORA, MLA_HEADS, MLA_NOPE + MLA_ROPE)
            wq_pe = wqb[..., MLA_NOPE:]
            wq_pe = jnp.concatenate([wq_pe, wq_pe[..., half:], wq_pe[..., :half]], axis=-1)
            wqb = jnp.concatenate([wqb[..., :MLA_NOPE].reshape(MLA_Q_LORA, -1),
                                   wq_pe.reshape(MLA_Q_LORA, -1)], axis=1).astype(BF16)
            wkvb = mla_w_kvb[e].astype(BF16)
            wkvb_h = mla_w_kvb[e].reshape(MLA_KV_LORA, MLA_HEADS, MLA_NOPE + MLA_VD)
            w_uk_t = jnp.transpose(wkvb_h[..., :MLA_NOPE], (1, 2, 0)).astype(BF16)
            w_uv = jnp.transpose(wkvb_h[..., MLA_NOPE:], (1, 0, 2)).astype(BF16)
            w_out = w_out_cd[e].astype(BF16)
            w_out1, w_out2 = w_out[:DIFF_HEADS * DIFF_VD], w_out[DIFF_HEADS * DIFF_VD:]
            qn, kvn = _row2(mla_q_norm[e]), _row2(mla_kv_norm[e])
            lam_p, subln = diff_lambda[e], _row2(diff_subln[e])

            tb_d, rb_d = _diff_tiles(seq)
            tb_m, rb_m = _mla_tiles(seq)
            dist = (jnp.arange(rb_d)[:, None] - jnp.arange(rb_d + REL_MAX_DIST)[None, :]) + REL_MAX_DIST
            n_keys = past_len + LANES
            kpos = jnp.arange(n_keys)
            dist_s = jnp.where(kpos < past_len, past_len - kpos, 0)
            bias_near, bias_s = _bias_tiles(
                rel_table, dist, _rel_bucket(jnp.maximum(dist, 0)),
                jnp.broadcast_to(_rel_bucket(dist_s)[None], (8, n_keys)))

            cs = _rope_table(jnp.arange(seq))
            (qd, dk, dv, dkb, dvx, ckv, kpe, qm, km, vmx) = _cd_in(
                hp, g_mix, w_in, qn, wqb, kvn, wkvb, cs, batch, seq, False)
            yd = _diff_attn(qd, dkb, dvx, bias_near, lam_p, subln, batch, seq, tb_d, rb_d, li)
            ym = _mla_attn(qm, km, vmx, batch, seq, tb_m, rb_m)
            hp = _mm_res([yd, ym], [w_out1, w_out2], hp)
            rows_p.append((dk.reshape(batch, seq, DIFF_KV_HEADS, 2 * DIFF_DH),
                           dv.reshape(batch, seq, DIFF_KV_HEADS, DIFF_VD),
                           ckv.reshape(batch, seq, MLA_KV_LORA), kpe.reshape(batch, seq, MLA_ROPE)))

            cs_s = jnp.broadcast_to(_rope_table(jnp.full((1,), past_len)), (nb, LANES))
            (qd_s, dk_s, dv_s, ckv_s, kpe_s, qlat, qpe) = _cd_in(
                hs, g_mix, w_in, qn, wqb, kvn, w_uk_t, cs_s, nb, 1, True)
            row_id = jnp.arange(DIFF_MAPS)
            qd_r = jnp.transpose(qd_s.reshape(nb, DIFF_KV_HEADS, DIFF_GROUP, 2, DIFF_DH), (0, 1, 3, 2, 4))
            qd_r = qd_r.reshape(nb, DIFF_MAPS, 1, DIFF_DH)
            slot = ((row_id // DIFF_GROUP) % 2)[:, None] == jnp.arange(2)[None, :]
            qd_pad = jnp.where(slot[None, :, :, None], qd_r, 0.0).reshape(nb, DIFF_MAPS, LANES).astype(BF16)
            pad = ((0, 0), (0, DIFF_MAPS - MLA_HEADS), (0, 0))
            ql = jnp.pad(qlat.reshape(nb, MLA_HEADS, MLA_KV_LORA), pad)
            qp = jnp.pad(qpe.reshape(nb, MLA_HEADS, LANES)[:, :, :MLA_ROPE], pad).astype(BF16)
            row_head = row_id // (2 * DIFF_GROUP)
            bias_rows = bias_s[row_head * DIFF_GROUP + row_id % DIFF_GROUP, 0]
            own = row_head[:, None] == (jnp.arange(past_len * DIFF_KV_HEADS) % DIFF_KV_HEADS)[None, :]
            bias_past = jnp.where(own, jnp.repeat(bias_rows[:, :past_len], DIFF_KV_HEADS, axis=1), NEG_INF)
            od, om = _decode(
                page_table.reshape(-1) * n_odd + e, qd_pad, ql, qp, bias_past, bias_rows[:, past_len:],
                dk_s[:, None], dv_s[:, None], ckv_s[:, None], kpe_s[:, None],
                cache_diff_k.reshape(-1, LANES), cache_diff_v.reshape(-1, LANES),
                cache_mla_ckv.reshape(-1, PAGE_SIZE, MLA_KV_LORA),
                jnp.swapaxes(cache_mla_kpe, 2, 3).reshape(-1, MLA_ROPE, PAGE_SIZE), n_pages)
            od = od.reshape(nb, DIFF_KV_HEADS, 2, DIFF_GROUP * DIFF_VD)
            o_map = [od[:, :, mp].reshape(nb, -1) for mp in range(2)]
            hs = _sample_out(o_map[0], o_map[1], om[:, :MLA_HEADS].reshape(nb, -1), lam_p, subln, w_uv,
                             w_out1, w_out2, hs, li)
            rows_s.append((dk_s.reshape(nb, 1, DIFF_KV_HEADS, 2 * DIFF_DH),
                           dv_s.reshape(nb, 1, DIFF_KV_HEADS, DIFF_VD),
                           ckv_s.reshape(nb, 1, MLA_KV_LORA), kpe_s.reshape(nb, 1, MLA_ROPE)))
        last = li == depth - 1
        hp = _ffn(hp, g_ffn, wu_all, wd_all, li, gf, last)
        hs = _ffn(hs, g_ffn, wu_all, wd_all, li, gf, last)

    y_prompt = hp.reshape(batch, seq, D_MODEL)
    y_sample = hs.reshape(nb, 1, D_MODEL)
    stack = lambda rows, i: jnp.stack([r[i] for r in rows], axis=1)
    return (y_prompt, y_sample, jnp.stack(pool_p, axis=0), jnp.stack(pool_s, axis=0), jnp.stack(sgv_s, axis=0),
            stack(rows_p, 0), stack(rows_p, 1), stack(rows_p, 2), stack(rows_p, 3),
            stack(rows_s, 0), stack(rows_s, 1), stack(rows_s, 2), stack(rows_s, 3))
```

```python
import functools
import math

import jax
import jax.numpy as jnp
from jax import lax
from jax.experimental import pallas as pl
from jax.experimental.pallas import tpu as pltpu

D_MODEL = 2048
PAGE_SIZE = 128
POOL_WINDOWS = (2, 4, 8, 16)
POOL_GROUPS = len(POOL_WINDOWS)
POOL_WIDTH = D_MODEL // 2
POOL_GW = POOL_WIDTH // POOL_GROUPS
POOL_BUF = max(POOL_WINDOWS) - 1
SG_CHUNK = 128
SG_GROUPS = 8
SG_WIDTH = D_MODEL // 2
SG_GW = SG_WIDTH // SG_GROUPS
DIFF_HEADS = 8
DIFF_KV_HEADS = 2
DIFF_GROUP = DIFF_HEADS // DIFF_KV_HEADS
DIFF_DH = 64
DIFF_VD = 2 * DIFF_DH
DIFF_SCALE = DIFF_DH ** -0.5
DIFF_MAPS = 2 * DIFF_HEADS
MLA_HEADS = 8
MLA_Q_LORA = 512
MLA_KV_LORA = 512
MLA_NOPE = 128
MLA_ROPE = 64
MLA_VD = 128
MLA_SCALE = (MLA_NOPE + MLA_ROPE) ** -0.5
ROPE_BASE = 10000.0
REL_BUCKETS = 32
REL_MAX_DIST = 128
D_FF = 4 * D_MODEL
EPS = 1e-6
NEG_INF = -1e30

LANES = 128
CD_Q = DIFF_HEADS * 2 * DIFF_DH
CD_K = DIFF_KV_HEADS * 2 * DIFF_DH
CD_V = DIFF_KV_HEADS * DIFF_VD
CD_IN_PAD = CD_Q + CD_K + CD_V + MLA_Q_LORA + MLA_KV_LORA + 2 * MLA_ROPE
MLA_QK = 2 * LANES
VX = 2 * LANES
LOG2E = math.log2(math.e)
DECODE_SLOTS = 4

BF16 = jnp.bfloat16
F32 = jnp.float32
VMEM_LIMIT_MB = 56


def _rms(x, g):
    return x * lax.rsqrt(jnp.mean(x * x, axis=-1, keepdims=True) + EPS) * g


def _dot(a, b):
    return jnp.dot(a, b, preferred_element_type=F32)


def _dot_nt(a, b):
    return lax.dot_general(a, b, (((1,), (1,)), ((), ())), preferred_element_type=F32)


def _params(n_axes):
    return pltpu.CompilerParams(dimension_semantics=("arbitrary",) * n_axes,
                                vmem_limit_bytes=VMEM_LIMIT_MB * 2 ** 20)


def _full(shape):
    zeros = (0,) * len(shape)
    return pl.BlockSpec(shape, lambda *_: zeros)


def _ffn_kernel(h_ref, g_ref, wu_ref, wd_ref, gf_ref, o_ref, xn_ref, *, final_norm):
    f = pl.program_id(1)

    @pl.when(f == 0)
    def _():
        h = h_ref[...]
        xn_ref[...] = _rms(h, g_ref[...]).astype(BF16)
        o_ref[...] = h

    a = jnp.maximum(_dot(xn_ref[...], wu_ref[0]), 0.0)
    o_ref[...] += _dot((a * a).astype(BF16), wd_ref[0])

    if final_norm:
        @pl.when(f == pl.num_programs(1) - 1)
        def _():
            o_ref[...] = _rms(o_ref[...], gf_ref[...])


def _ffn(h, g, wu, wd, li, gf, final_norm):
    m = h.shape[0]
    tm, tf = min(m, 512), 1024
    return pl.pallas_call(
        functools.partial(_ffn_kernel, final_norm=final_norm),
        grid=(m // tm, D_FF // tf),
        in_specs=[pl.BlockSpec((tm, D_MODEL), lambda i, f: (i, 0)),
                  _full((1, D_MODEL)),
                  pl.BlockSpec((1, D_MODEL, tf), lambda i, f: (li, 0, f)),
                  pl.BlockSpec((1, tf, D_MODEL), lambda i, f: (li, f, 0)),
                  _full((1, D_MODEL))],
        out_specs=pl.BlockSpec((tm, D_MODEL), lambda i, f: (i, 0)),
        out_shape=jax.ShapeDtypeStruct((m, D_MODEL), F32),
        scratch_shapes=[pltpu.VMEM((tm, D_MODEL), BF16)],
        compiler_params=_params(2), name="ffn")(h, g, wu, wd, gf)


def _mm_res_kernel(*refs, n):
    acc = refs[2 * n][...]
    for y_ref, w_ref in zip(refs[:n], refs[n:2 * n]):
        acc = acc + _dot(y_ref[...], w_ref[...])
    refs[2 * n + 1][...] = acc


def _mm_res(ys, ws, res):
    m = res.shape[0]
    tm = min(m, 512)
    n = len(ys)
    return pl.pallas_call(
        functools.partial(_mm_res_kernel, n=n),
        grid=(m // tm,),
        in_specs=([pl.BlockSpec((tm, y.shape[1]), lambda i: (i, 0)) for y in ys]
                  + [_full(w.shape) for w in ws]
                  + [pl.BlockSpec((tm, D_MODEL), lambda i: (i, 0))]),
        out_specs=pl.BlockSpec((tm, D_MODEL), lambda i: (i, 0)),
        out_shape=jax.ShapeDtypeStruct((m, D_MODEL), F32),
        compiler_params=_params(1), name="mm_res")(*ys, *ws, res)


def _ab_in_kernel(h_ref, g_ref, w_ref, sgn_ref, o_ref):
    xn = _rms(h_ref[...], g_ref[...]).astype(BF16)
    z_hi, u_hi = POOL_WIDTH, POOL_WIDTH + SG_WIDTH
    o_ref[:, :z_hi] = _dot(xn, w_ref[:, :z_hi])
    o_ref[:, z_hi:u_hi] = jax.nn.gelu(_dot(xn, w_ref[:, z_hi:u_hi]))
    o_ref[:, u_hi:] = _rms(jax.nn.gelu(_dot(xn, w_ref[:, u_hi:])), sgn_ref[...])


def _ab_in(h, g, w, sgn):
    m = h.shape[0]
    tm = min(m, 256)
    width = w.shape[1]
    return pl.pallas_call(
        _ab_in_kernel,
        grid=(m // tm,),
        in_specs=[pl.BlockSpec((tm, D_MODEL), lambda i: (i, 0)), _full((1, D_MODEL)), _full(w.shape),
                  _full((1, SG_WIDTH))],
        out_specs=pl.BlockSpec((tm, width), lambda i: (i, 0)),
        out_shape=jax.ShapeDtypeStruct((m, width), F32),
        compiler_params=_params(1), name="ab_in")(h, g, w, sgn)


def _ab_mix_kernel(z_ref, u_ref, v_ref, pw_ref, ps_ref, sw_ref, sbt_ref, o_ref, zbuf_ref, *, tm):
    t = pl.program_id(1)
    carry = 16

    @pl.when(t == 0)
    def _():
        zbuf_ref[0:carry, :] = jnp.zeros((carry, POOL_WIDTH), F32)

    z = z_ref[...]
    zbuf_ref[carry:carry + tm, :] = z
    pos = t * tm + lax.broadcasted_iota(jnp.int32, (tm, 1), 0)
    for gi, w in enumerate(POOL_WINDOWS):
        lo, hi = gi * POOL_GW, (gi + 1) * POOL_GW
        s = z[:, lo:hi]
        for i in range(1, w):
            s = s + zbuf_ref[pl.ds(carry - i, tm), lo:hi]
        cnt = jnp.minimum(pos + 1, w).astype(F32)
        pooled = s / cnt - z[:, lo:hi]
        ya = _dot(pooled.astype(BF16), pw_ref[gi]) * ps_ref[:, lo:hi]
        o_ref[:, lo:hi] = ya.astype(BF16)
    zbuf_ref[0:carry, :] = zbuf_ref[tm:tm + carry, :]

    row = lax.broadcasted_iota(jnp.int32, (SG_CHUNK, SG_CHUNK), 0)
    col = lax.broadcasted_iota(jnp.int32, (SG_CHUNK, SG_CHUNK), 1)
    for g in range(SG_GROUPS):
        wg = jnp.where(row >= col, sw_ref[g], 0.0).astype(BF16)
        bg = sbt_ref[:, g:g + 1]
        lo, hi = g * SG_GW, (g + 1) * SG_GW
        for c in range(tm // SG_CHUNK):
            r0, r1 = c * SG_CHUNK, (c + 1) * SG_CHUNK
            mixed = _dot(wg, v_ref[r0:r1, lo:hi].astype(BF16)) + bg
            o_ref[r0:r1, POOL_WIDTH + lo:POOL_WIDTH + hi] = (u_ref[r0:r1, lo:hi] * mixed).astype(BF16)


def _ab_mix(zuv, batch, seq, pool_w, pool_scale, sg_w, sg_bt):
    tm = min(seq, 512)
    nt = seq // tm
    col = lambda c: pl.BlockSpec((tm, POOL_WIDTH), lambda b, t: (b * nt + t, c))
    return pl.pallas_call(
        functools.partial(_ab_mix_kernel, tm=tm),
        grid=(batch, nt),
        in_specs=[col(0), col(1), col(2),
                  _full(pool_w.shape), _full(pool_scale.shape), _full(sg_w.shape), _full(sg_bt.shape)],
        out_specs=pl.BlockSpec((tm, D_MODEL), lambda b, t: (b * nt + t, 0)),
        out_shape=jax.ShapeDtypeStruct((batch * seq, D_MODEL), BF16),
        scratch_shapes=[pltpu.VMEM((tm + 16, POOL_WIDTH), F32)],
        compiler_params=_params(2), name="ab_mix")(zuv, zuv, zuv, pool_w, pool_scale, sg_w, sg_bt)


def _ab_mix_s_kernel(zuv_ref, st_ref, pw_ref, ps_ref, w0_ref, b0_ref, o_ref, *, cnts):
    z = zuv_ref[:, 0:POOL_WIDTH]
    for gi, w in enumerate(POOL_WINDOWS):
        lo, hi = gi * POOL_GW, (gi + 1) * POOL_GW
        s = z[:, lo:hi]
        for i in range(1, w):
            s = s + st_ref[POOL_BUF - i, :, lo:hi]
        pooled = s / cnts[gi] - z[:, lo:hi]
        ya = _dot(pooled.astype(BF16), pw_ref[gi]) * ps_ref[:, lo:hi]
        o_ref[:, lo:hi] = ya.astype(BF16)
    u = zuv_ref[:, POOL_WIDTH:POOL_WIDTH + SG_WIDTH]
    v = zuv_ref[:, POOL_WIDTH + SG_WIDTH:]
    o_ref[:, POOL_WIDTH:] = (u * (w0_ref[...] * v + b0_ref[...])).astype(BF16)


def _ab_mix_s(zuv, state, pool_w, pool_scale, w0, b0, cnts):
    m = zuv.shape[0]
    return pl.pallas_call(
        functools.partial(_ab_mix_s_kernel, cnts=cnts),
        grid=(1,),
        in_specs=[_full(zuv.shape), _full(state.shape), _full(pool_w.shape), _full(pool_scale.shape),
                  _full(w0.shape), _full(b0.shape)],
        out_specs=_full((m, D_MODEL)),
        out_shape=jax.ShapeDtypeStruct((m, D_MODEL), BF16),
        compiler_params=_params(1), name="ab_mix_s")(zuv, state, pool_w, pool_scale, w0, b0)


def _rope128(t, cs):
    u = t * cs
    return u + pltpu.roll(u, 2 * 32, axis=1)


def _cd_in_kernel(h_ref, g_ref, win_ref, qn_ref, wqb_ref, kvn_ref, wkv_ref, cs_ref, *outs, sample):
    xn = _rms(h_ref[...], g_ref[...]).astype(BF16)
    p = _dot(xn, win_ref[...])
    tm = p.shape[0]
    lane = lax.broadcasted_iota(jnp.int32, (tm, LANES), 1)
    first = lane < MLA_ROPE
    cs = cs_ref[...]
    o_k, o_v = CD_Q, CD_Q + CD_K
    o_cq, o_ckv, o_pe = o_v + CD_V, o_v + CD_V + MLA_Q_LORA, o_v + CD_V + MLA_Q_LORA + MLA_KV_LORA
    dk, dv = p[:, o_k:o_v], p[:, o_v:o_cq]
    cqn = _rms(p[:, o_cq:o_ckv], qn_ref[...]).astype(BF16)
    ckv = _rms(p[:, o_ckv:o_pe], kvn_ref[...])
    rk = _rope128(p[:, o_pe:o_pe + LANES], cs)
    q = _dot(cqn, wqb_ref[...])
    q_nope = MLA_HEADS * MLA_NOPE
    if sample:
        qd_ref, dk_ref, dv_ref, ckv_ref, kpe_ref, qlat_ref, qpe_ref = outs
        qd_ref[...] = p[:, 0:CD_Q] * DIFF_SCALE
        for h in range(MLA_HEADS):
            qn = q[:, h * MLA_NOPE:(h + 1) * MLA_NOPE].astype(BF16)
            qlat_ref[:, h * MLA_KV_LORA:(h + 1) * MLA_KV_LORA] = _dot(qn, wkv_ref[h]).astype(BF16)
            qpe_ref[:, h * LANES:(h + 1) * LANES] = _rope128(
                q[:, q_nope + h * LANES:q_nope + (h + 1) * LANES], cs)
    else:
        qd_ref, dk_ref, dv_ref, dkb_ref, dvx_ref, ckv_ref, kpe_ref, qm_ref, km_ref, vmx_ref = outs
        ones = jnp.ones((tm, LANES), BF16)
        for hg in range(DIFF_HEADS):
            slot = p[:, hg * LANES:(hg + 1) * LANES] * (DIFF_SCALE * LOG2E)
            h, g = divmod(hg, DIFF_GROUP)
            qd_ref[0, (h * 2 + 0) * DIFF_GROUP + g] = jnp.where(first, slot, 0.0).astype(BF16)
            qd_ref[0, (h * 2 + 1) * DIFF_GROUP + g] = jnp.where(first, 0.0, slot).astype(BF16)
        dkb_ref[...] = dk.astype(BF16)
        for h in range(DIFF_KV_HEADS):
            dvx_ref[:, h * VX:h * VX + DIFF_VD] = dv[:, h * DIFF_VD:(h + 1) * DIFF_VD].astype(BF16)
            dvx_ref[:, h * VX + DIFF_VD:(h + 1) * VX] = ones
        kv = _dot(ckv.astype(BF16), wkv_ref[...])
        kpe_slot = jnp.where(first, rk, 0.0).astype(BF16)
        q = q * (MLA_SCALE * LOG2E)
        for h in range(MLA_HEADS):
            qm_ref[:, h * MLA_QK:h * MLA_QK + LANES] = q[:, h * MLA_NOPE:(h + 1) * MLA_NOPE].astype(BF16)
            qm_ref[:, h * MLA_QK + LANES:(h + 1) * MLA_QK] = _rope128(
                q[:, q_nope + h * LANES:q_nope + (h + 1) * LANES], cs).astype(BF16)
            km_ref[:, h * MLA_QK:h * MLA_QK + LANES] = kv[:, h * 2 * LANES:h * 2 * LANES + LANES].astype(BF16)
            km_ref[:, h * MLA_QK + LANES:(h + 1) * MLA_QK] = kpe_slot
            vmx_ref[:, h * VX:h * VX + MLA_VD] = kv[:, h * 2 * LANES + LANES:(h + 1) * 2 * LANES].astype(BF16)
            vmx_ref[:, h * VX + MLA_VD:(h + 1) * VX] = ones
    dk_ref[...] = dk
    dv_ref[...] = dv
    ckv_ref[...] = ckv
    kpe_ref[...] = rk[:, 0:MLA_ROPE]


def _cd_in(h, g, win, qn, wqb, kvn, wkv, cs, batch, seq, sample):
    m = h.shape[0]
    tm = min(seq, 256) if not sample else m
    nt = seq // tm if not sample else 1
    row = lambda width, dt: (pl.BlockSpec((tm, width), lambda i: (i, 0)), jax.ShapeDtypeStruct((m, width), dt))
    common = [row(CD_K, F32), row(CD_V, F32)]
    tail = [row(MLA_KV_LORA, F32), row(MLA_ROPE, F32)]
    if sample:
        outs = ([row(CD_Q, F32)] + common + tail
                + [row(MLA_HEADS * MLA_KV_LORA, BF16), row(MLA_HEADS * LANES, F32)])
        cs_spec = _full(cs.shape)
    else:
        qd = (pl.BlockSpec((1, DIFF_MAPS, tm, LANES), lambda i: (i // nt, 0, i % nt, 0)),
              jax.ShapeDtypeStruct((batch, DIFF_MAPS, seq, LANES), BF16))
        outs = ([qd] + common + [row(CD_K, BF16), row(DIFF_KV_HEADS * VX, BF16)] + tail
                + [row(MLA_HEADS * MLA_QK, BF16), row(MLA_HEADS * MLA_QK, BF16), row(MLA_HEADS * VX, BF16)])
        cs_spec = pl.BlockSpec((tm, LANES), lambda i: (i % nt, 0))
    return pl.pallas_call(
        functools.partial(_cd_in_kernel, sample=sample),
        grid=(m // tm,),
        in_specs=[pl.BlockSpec((tm, D_MODEL), lambda i: (i, 0)), _full(g.shape), _full(win.shape),
                  _full(qn.shape), _full(wqb.shape), _full(kvn.shape), _full(wkv.shape), cs_spec],
        out_specs=[o[0] for o in outs],
        out_shape=[o[1] for o in outs],
        compiler_params=_params(1), name="cd_in_s" if sample else "cd_in")(h, g, win, qn, wqb, kvn, wkv, cs)


def _bias_kernel(tab_ref, dist_ref, bn_ref, bs_ref, on_ref, os_ref):
    def lookup(bkt, head):
        val = jnp.zeros(bkt.shape, F32)
        for b in range(REL_BUCKETS):
            val = jnp.where(bkt == b, tab_ref[b, head], val)
        return val

    future = dist_ref[...] < 0
    bn, bs = bn_ref[...], bs_ref[...]
    for head in range(DIFF_HEADS):
        far = tab_ref[REL_BUCKETS - 1, head]
        on_ref[head] = jnp.where(future, NEG_INF, (lookup(bn, head) - far) * LOG2E)
        os_ref[head] = lookup(bs, head)


def _bias_tiles(rel_table, dist_near, bkt_near, bkt_sample):
    vmem = pl.BlockSpec(memory_space=pltpu.VMEM)
    return pl.pallas_call(
        _bias_kernel,
        in_specs=[pl.BlockSpec(memory_space=pltpu.SMEM), vmem, vmem, vmem],
        out_specs=[vmem, vmem],
        out_shape=[jax.ShapeDtypeStruct((DIFF_HEADS,) + bkt_near.shape, F32),
                   jax.ShapeDtypeStruct((DIFF_HEADS,) + bkt_sample.shape, F32)],
        name="rel_bias")(rel_table, dist_near, bkt_near, bkt_sample)


def _attn_block(q, k, vx, m_ref, acc_ref, row0, near=None):
    rb, c = q.shape[0], k.shape[0]
    s = _dot_nt(q, k)
    if near is not None:
        s = near(s)
    rows = pl.ds(row0, rb)
    m_prev = m_ref[rows, :]
    m_new = jnp.maximum(m_prev, jnp.max(s, axis=-1, keepdims=True))
    alpha = jnp.exp2(m_prev - m_new)
    p = jnp.exp2(s - jnp.concatenate([m_new] * (c // LANES), axis=1)).astype(BF16)
    acc_ref[rows, :] = jnp.concatenate([alpha] * (VX // LANES), axis=1) * acc_ref[rows, :] + _dot(p, vx)
    m_ref[rows, :] = m_new


def _attn_init(m_ref, acc_ref):
    m_ref[...] = jnp.full(m_ref.shape, NEG_INF, F32)
    acc_ref[...] = jnp.zeros(acc_ref.shape, F32)


def _causal_pairs(n):
    pairs = [(i, j) for i in range(n) for j in range(i + 1)]
    return (jnp.asarray([p[0] for p in pairs], jnp.int32), jnp.asarray([p[1] for p in pairs], jnp.int32))


def _softmax_init(m_ref, l_ref, acc_ref):
    m_ref[...] = jnp.full(m_ref.shape, NEG_INF, F32)
    l_ref[...] = jnp.zeros(l_ref.shape, F32)
    acc_ref[...] = jnp.zeros(acc_ref.shape, F32)


def _diff_lambda(lp, li):
    lam_init = 0.8 - 0.6 * math.exp(-0.3 * li)
    a = jnp.sum(lp[0:1, :] * lp[1:2, :], axis=-1, keepdims=True)
    b = jnp.sum(lp[2:3, :] * lp[3:4, :], axis=-1, keepdims=True)
    return jnp.exp(a) - jnp.exp(b) + lam_init, lam_init


def _diff_attn_kernel(ii_ref, jj_ref, q_ref, k_ref, vx_ref, near_ref, lam_ref, sub_ref, o_ref,
                      m_ref, acc_ref, *, tb, rb, li):
    t = pl.program_id(2)
    i, j = ii_ref[t], jj_ref[t]
    slots = 2 * DIFF_GROUP
    nrb = tb // rb
    band = REL_MAX_DIST

    @pl.when(j == 0)
    def _():
        _attn_init(m_ref, acc_ref)

    def tile(kind):
        k, vx = k_ref[...], vx_ref[...]
        for slot in range(slots):
            g = slot % DIFF_GROUP
            for b in range(nrb):
                r0 = b * rb
                q = q_ref[0, slot, r0:r0 + rb, :]
                kk, vv, near = k, vx, None
                if kind == "prev" and b == 0:
                    near = lambda s, g=g: jnp.concatenate(
                        [s[:, :tb - band], s[:, tb - band:] + near_ref[g, :, 0:band]], axis=1)
                elif kind == "diag":
                    kk, vv = k[:r0 + rb], vx[:r0 + rb]
                    if b == 0:
                        near = lambda s, g=g: s + near_ref[g, :, band:]
                    elif r0 == band:
                        near = lambda s, g=g: s + near_ref[g]
                    else:
                        near = lambda s, g=g, r0=r0: jnp.concatenate(
                            [s[:, :r0 - band], s[:, r0 - band:] + near_ref[g]], axis=1)
                _attn_block(q, kk, vv, m_ref, acc_ref, slot * tb + r0, near)

    @pl.when(j < i - 1)
    def _():
        tile("far")

    @pl.when(j == i - 1)
    def _():
        tile("prev")

    @pl.when(j == i)
    def _():
        tile("diag")
        lam, lam_init = _diff_lambda(lam_ref[...], li)
        for g in range(DIFF_GROUP):
            a0 = acc_ref[g * tb:(g + 1) * tb, :]
            a1 = acc_ref[(DIFF_GROUP + g) * tb:(DIFF_GROUP + g + 1) * tb, :]
            y = a0[:, :DIFF_VD] / a0[:, DIFF_VD:] - lam * (a1[:, :DIFF_VD] / a1[:, DIFF_VD:])
            o_ref[:, g * DIFF_VD:(g + 1) * DIFF_VD] = (_rms(y, sub_ref[...]) * (1.0 - lam_init)).astype(BF16)


def _diff_attn(qd, kb, vx, near, lam_p, subln, batch, seq, tb, rb, li):
    n = seq // tb
    slots = 2 * DIFF_GROUP
    ii, jj = _causal_pairs(n)
    grid_spec = pltpu.PrefetchScalarGridSpec(
        num_scalar_prefetch=2, grid=(batch, DIFF_KV_HEADS, ii.shape[0]),
        in_specs=[pl.BlockSpec((1, slots, tb, LANES), lambda b, h, t, ii, jj: (b, h, ii[t], 0)),
                  pl.BlockSpec((tb, LANES), lambda b, h, t, ii, jj: (b * n + jj[t], h)),
                  pl.BlockSpec((tb, VX), lambda b, h, t, ii, jj: (b * n + jj[t], h)),
                  pl.BlockSpec((DIFF_GROUP,) + near.shape[1:], lambda b, h, t, ii, jj: (h, 0, 0)),
                  pl.BlockSpec(lam_p.shape, lambda b, h, t, ii, jj: (0, 0)),
                  pl.BlockSpec(subln.shape, lambda b, h, t, ii, jj: (0, 0))],
        out_specs=pl.BlockSpec((tb, DIFF_GROUP * DIFF_VD), lambda b, h, t, ii, jj: (b * n + ii[t], h)),
        scratch_shapes=[pltpu.VMEM((slots * tb, LANES), F32), pltpu.VMEM((slots * tb, VX), F32)])
    return pl.pallas_call(
        functools.partial(_diff_attn_kernel, tb=tb, rb=rb, li=li),
        grid_spec=grid_spec,
        out_shape=jax.ShapeDtypeStruct((batch * seq, DIFF_HEADS * DIFF_VD), BF16),
        compiler_params=_params(3), name="diff_attn")(ii, jj, qd, kb, vx, near, lam_p, subln)


def _mla_attn_kernel(ii_ref, jj_ref, q_ref, k_ref, vx_ref, o_ref, m_ref, acc_ref, *, tb, rb):
    t = pl.program_id(2)
    i, j = ii_ref[t], jj_ref[t]

    @pl.when(j == 0)
    def _():
        _attn_init(m_ref, acc_ref)

    def tile(diag):
        k, vx = k_ref[...], vx_ref[...]
        causal = (lax.broadcasted_iota(jnp.int32, (rb, rb), 0) >= lax.broadcasted_iota(jnp.int32, (rb, rb), 1))
        for b in range(tb // rb):
            r0 = b * rb
            q = q_ref[r0:r0 + rb, :]
            if not diag:
                _attn_block(q, k, vx, m_ref, acc_ref, r0)
            elif b == 0:
                _attn_block(q, k[:rb], vx[:rb], m_ref, acc_ref, r0, lambda s: jnp.where(causal, s, NEG_INF))
            else:
                near = lambda s, r0=r0: jnp.concatenate(
                    [s[:, :r0], jnp.where(causal, s[:, r0:], NEG_INF)], axis=1)
                _attn_block(q, k[:r0 + rb], vx[:r0 + rb], m_ref, acc_ref, r0, near)

    @pl.when(j < i)
    def _():
        tile(False)

    @pl.when(j == i)
    def _():
        tile(True)
        acc = acc_ref[...]
        o_ref[...] = (acc[:, :MLA_VD] / acc[:, MLA_VD:]).astype(BF16)


def _mla_attn(qm, km, vmx, batch, seq, tb, rb):
    n = seq // tb
    ii, jj = _causal_pairs(n)
    grid_spec = pltpu.PrefetchScalarGridSpec(
        num_scalar_prefetch=2, grid=(batch, MLA_HEADS, ii.shape[0]),
        in_specs=[pl.BlockSpec((tb, MLA_QK), lambda b, h, t, ii, jj: (b * n + ii[t], h)),
                  pl.BlockSpec((tb, MLA_QK), lambda b, h, t, ii, jj: (b * n + jj[t], h)),
                  pl.BlockSpec((tb, VX), lambda b, h, t, ii, jj: (b * n + jj[t], h))],
        out_specs=pl.BlockSpec((tb, MLA_VD), lambda b, h, t, ii, jj: (b * n + ii[t], h)),
        scratch_shapes=[pltpu.VMEM((tb, LANES), F32), pltpu.VMEM((tb, VX), F32)])
    return pl.pallas_call(
        functools.partial(_mla_attn_kernel, tb=tb, rb=rb),
        grid_spec=grid_spec,
        out_shape=jax.ShapeDtypeStruct((batch * seq, MLA_HEADS * MLA_VD), BF16),
        compiler_params=_params(3), name="mla_attn")(ii, jj, qm, km, vmx)


def _decode_kernel(pt_ref, qd_ref, ql_ref, qp_ref, bias_ref, bnew_ref, kdn_ref, vdn_ref, ckvn_ref, kpen_ref,
                   cdk_ref, cdv_ref, cckv_ref, ckpe_ref, od_ref, om_ref, *rest, pages, nch):
    nslot = DECODE_SLOTS
    bufs = tuple(rest[4 * i:4 * i + 4] for i in range(nslot))
    sem, md_ref, ld_ref, accd_ref, mm_ref, lm_ref, accm_ref = rest[4 * nslot:]
    s = pl.program_id(0)
    last = pl.num_programs(0) - 1
    total = pl.num_programs(0) * nslot
    c = lax.rem(s, nch)
    drows = PAGE_SIZE * DIFF_KV_HEADS

    def page_copies(chunk, slot):
        out = []
        for k in range(pages):
            pg = 0 if chunk is None else pt_ref[chunk * pages + k]
            row = pl.multiple_of(pg * drows, drows)
            srcs = (cdk_ref.at[pl.ds(row, drows), :], cdv_ref.at[pl.ds(row, drows), :],
                    cckv_ref.at[pg], ckpe_ref.at[pg])
            for a, src in enumerate(srcs):
                out.append(pltpu.make_async_copy(src, bufs[slot][a].at[k], sem.at[slot, a]))
        return out

    def start(chunk, slot):
        for n, cp in enumerate(page_copies(chunk, slot)):
            cp.start(priority=(n // 4) % 2)

    def wait(slot):
        for cp in page_copies(None, slot):
            cp.wait()

    @pl.when(s == 0)
    def _():
        for j in range(nslot - 1):
            start(j, j)

    @pl.when(c == 0)
    def _():
        _softmax_init(md_ref, ld_ref, accd_ref)
        _softmax_init(mm_ref, lm_ref, accm_ref)

    qd, ql, qp = qd_ref[0], ql_ref[0], qp_ref[0]

    def update(s, vals, m_ref, l_ref, acc_ref):
        m_prev = m_ref[...]
        m_new = jnp.maximum(m_prev, jnp.max(s, axis=-1, keepdims=True))
        alpha = jnp.exp(m_prev - m_new)
        p = jnp.exp(s - m_new)
        l_ref[...] = alpha * l_ref[...] + jnp.sum(p, axis=-1, keepdims=True)
        pb = p.astype(BF16)
        acc = alpha * acc_ref[...]
        width = vals[0].shape[0]
        for k, val in enumerate(vals):
            acc = acc + _dot(pb[:, k * width:(k + 1) * width], val)
        acc_ref[...] = acc
        m_ref[...] = m_new

    def attend(slot):
        kd, vd, ckv, kpe = bufs[slot]
        vds = [vd[k].astype(BF16) for k in range(pages)]
        ckvs = [ckv[k].astype(BF16) for k in range(pages)]
        sd = (jnp.concatenate([_dot_nt(qd, kd[k].astype(BF16)) for k in range(pages)], axis=1)
              + bias_ref[:, slot * pages * drows:(slot + 1) * pages * drows])
        sm = jnp.concatenate([_dot_nt(ql, ckvs[k]) + _dot(qp, kpe[k].astype(BF16))
                              for k in range(pages)], axis=1) * MLA_SCALE
        update(sd, vds, md_ref, ld_ref, accd_ref)
        update(sm, ckvs, mm_ref, lm_ref, accm_ref)

    for j in range(nslot):
        wait(j)
        start(lax.rem(s * nslot + j + nslot - 1, total), (j + nslot - 1) % nslot)
        attend(j)

    @pl.when(s == last)
    def _():
        for j in range(nslot - 1):
            wait(j)

    @pl.when(c == nch - 1)
    def _():
        def finish(s_new, v_new, m_ref, l_ref, acc_ref, o_ref):
            m_prev = m_ref[...]
            m_new = jnp.maximum(m_prev, s_new)
            alpha = jnp.exp(m_prev - m_new)
            p = jnp.exp(s_new - m_new)
            l = alpha * l_ref[...] + p
            o_ref[0] = (alpha * acc_ref[...] + p * v_new) / l

        head0 = lax.broadcasted_iota(jnp.int32, (DIFF_MAPS, LANES), 0) < 2 * DIFF_GROUP
        kdn, vdn = kdn_ref[0], vdn_ref[0]
        kd_new = jnp.where(head0, kdn[:, 0:LANES], kdn[:, LANES:])
        vd_new = jnp.where(head0, vdn[:, 0:DIFF_VD], vdn[:, DIFF_VD:])
        sd_new = jnp.sum(qd.astype(F32) * kd_new, axis=-1, keepdims=True) + bnew_ref[:, 0:1]
        finish(sd_new, vd_new, md_ref, ld_ref, accd_ref, od_ref)
        sm_new = (jnp.sum(ql.astype(F32) * ckvn_ref[0], axis=-1, keepdims=True)
                  + jnp.sum(qp.astype(F32) * kpen_ref[0], axis=-1, keepdims=True)) * MLA_SCALE
        finish(sm_new, ckvn_ref[0], mm_ref, lm_ref, accm_ref, om_ref)


def _decode(page_idx, qd, ql, qp, bias, bias_new, kd_new, vd_new, ckv_new, kpe_new, c_dk, c_dv, c_ckv, c_kpe,
            n_pages):
    nb = qd.shape[0]
    nslot = DECODE_SLOTS
    pages = next(p for p in (8, 4, 2, 1) if n_pages % (nslot * p) == 0)
    nch = n_pages // (nslot * pages)
    rows = DIFF_MAPS
    drows = PAGE_SIZE * DIFF_KV_HEADS
    per_b = lambda shape: pl.BlockSpec((1,) + shape, lambda s, pt: (s // nch, 0, 0))
    hbm = pl.BlockSpec(memory_space=pl.ANY)
    page_bufs = [pltpu.VMEM((pages, drows, LANES), F32), pltpu.VMEM((pages, drows, LANES), F32),
                 pltpu.VMEM((pages, PAGE_SIZE, MLA_KV_LORA), F32), pltpu.VMEM((pages, MLA_ROPE, PAGE_SIZE), F32)]
    in_specs = [per_b((rows, LANES)), per_b((rows, MLA_KV_LORA)), per_b((rows, MLA_ROPE)),
                pl.BlockSpec((rows, nslot * pages * drows), lambda s, pt: (0, s % nch)),
                pl.BlockSpec(bias_new.shape, lambda s, pt: (0, 0)),
                per_b((1, CD_K)), per_b((1, CD_V)), per_b((1, MLA_KV_LORA)), per_b((1, MLA_ROPE)),
                hbm, hbm, hbm, hbm]
    grid_spec = pltpu.PrefetchScalarGridSpec(
        num_scalar_prefetch=1, grid=(nb * nch,), in_specs=in_specs,
        out_specs=[per_b((rows, DIFF_VD)), per_b((rows, MLA_KV_LORA))],
        scratch_shapes=(page_bufs * nslot + [pltpu.SemaphoreType.DMA((nslot, 4))]
                        + [pltpu.VMEM((rows, 1), F32), pltpu.VMEM((rows, 1), F32), pltpu.VMEM((rows, DIFF_VD), F32),
                           pltpu.VMEM((rows, 1), F32), pltpu.VMEM((rows, 1), F32),
                           pltpu.VMEM((rows, MLA_KV_LORA), F32)]))
    return pl.pallas_call(
        functools.partial(_decode_kernel, pages=pages, nch=nch),
        grid_spec=grid_spec,
        out_shape=[jax.ShapeDtypeStruct((nb, rows, DIFF_VD), F32),
                   jax.ShapeDtypeStruct((nb, rows, MLA_KV_LORA), F32)],
        compiler_params=_params(1), name="decode_attn")(
            page_idx, qd, ql, qp, bias, bias_new, kd_new, vd_new, ckv_new, kpe_new, c_dk, c_dv, c_ckv, c_kpe)


def _sample_out_kernel(o0_ref, o1_ref, om_ref, lam_ref, sub_ref, wuv_ref, w1_ref, w2_ref, res_ref, o_ref, *, li):
    lam, lam_init = _diff_lambda(lam_ref[...], li)
    acc = res_ref[...]
    for hd in range(DIFF_HEADS):
        lo, hi = hd * DIFF_VD, (hd + 1) * DIFF_VD
        y = o0_ref[:, lo:hi] - lam * o1_ref[:, lo:hi]
        yd = (_rms(y, sub_ref[...]) * (1.0 - lam_init)).astype(BF16)
        acc = acc + _dot(yd, w1_ref[lo:hi, :])
    for h in range(MLA_HEADS):
        om = om_ref[:, h * MLA_KV_LORA:(h + 1) * MLA_KV_LORA].astype(BF16)
        ym = _dot(om, wuv_ref[h]).astype(BF16)
        acc = acc + _dot(ym, w2_ref[h * MLA_VD:(h + 1) * MLA_VD, :])
    o_ref[...] = acc


def _sample_out(o0, o1, om, lam_p, subln, wuv, w1, w2, res, li):
    args = (o0, o1, om, lam_p, subln, wuv, w1, w2, res)
    return pl.pallas_call(
        functools.partial(_sample_out_kernel, li=li),
        grid=(1,),
        in_specs=[_full(a.shape) for a in args],
        out_specs=_full(res.shape),
        out_shape=jax.ShapeDtypeStruct(res.shape, F32),
        compiler_params=_params(1), name="sample_out")(*args)


def _rope_table(pos):
    half = MLA_ROPE // 2
    freq = ROPE_BASE ** (-jnp.arange(half, dtype=F32) / half)
    ang = pos.astype(F32)[:, None] * freq[None, :]
    c, s = jnp.cos(ang), jnp.sin(ang)
    return jnp.concatenate([c, c, -s, s], axis=-1)


def _rel_bucket(n):
    exact = REL_BUCKETS // 2
    nf = jnp.maximum(n, 1).astype(F32)
    large = exact + (jnp.log(nf / exact) / math.log(REL_MAX_DIST / exact)
                     * (REL_BUCKETS - exact)).astype(jnp.int32)
    return jnp.where(n < exact, n, jnp.minimum(large, REL_BUCKETS - 1))


def _row2(x):
    return x.reshape(1, -1)


def _diff_tiles(seq):
    tb = min(seq, 512)
    return tb, tb


def _mla_tiles(seq):
    tb = min(seq, 1024)
    return tb, min(tb, 256)


def kernel(x_prompt, x_sample, state_pool, cache_diff_k, cache_diff_v, cache_mla_ckv, cache_mla_kpe, page_table, norm_mix, norm_ffn, norm_final, w_ffn_up, w_ffn_down, w_in_ab, pool_w, pool_scale, sg_norm, sg_w, sg_b, w_out_ab, w_in_cd, diff_lambda, diff_subln, mla_q_norm, mla_w_qb, mla_kv_norm, mla_w_kvb, w_out_cd, rel_table):
    batch, seq, _ = x_prompt.shape
    nb, dec_seq, _ = x_sample.shape
    assert dec_seq == 1
    n_pages = page_table.shape[1]
    past_len = n_pages * PAGE_SIZE
    depth = norm_mix.shape[0]
    n_odd = cache_diff_k.shape[1]
    hp = x_prompt.reshape(batch * seq, D_MODEL)
    hs = x_sample.reshape(nb, D_MODEL)
    gf = _row2(norm_final)
    wu_all, wd_all = w_ffn_up.astype(BF16), w_ffn_down.astype(BF16)

    pool_p, pool_s, sgv_s, rows_p, rows_s = [], [], [], [], []
    for li in range(depth):
        e = li // 2
        g_mix, g_ffn = _row2(norm_mix[li]), _row2(norm_ffn[li])
        if li % 2 == 0:
            w_in = w_in_ab[e].astype(BF16)
            pw = pool_w[e].astype(BF16)
            ps = _row2(pool_scale[e])
            sgn = _row2(sg_norm[e])
            w_out = w_out_ab[e].astype(BF16)
            zuv = _ab_in(hp, g_mix, w_in, sgn)
            yab = _ab_mix(zuv, batch, seq, pw, ps, sg_w[e], sg_b[e, :, :SG_CHUNK].T)
            pool_p.append(zuv.reshape(batch, seq, -1)[:, seq - POOL_BUF:, :POOL_WIDTH])
            hp = _mm_res([yab], [w_out], hp)
            zuv_s = _ab_in(hs, g_mix, w_in, sgn)
            w0 = _row2(jnp.repeat(sg_w[e, :, 0, 0], SG_GW))
            b0 = _row2(jnp.repeat(sg_b[e, :, 0], SG_GW))
            cnts = tuple(float(min(past_len + 1, w)) for w in POOL_WINDOWS)
            yab_s = _ab_mix_s(zuv_s, jnp.transpose(state_pool[e], (1, 0, 2)), pw, ps, w0, b0, cnts)
            pool_s.append(jnp.concatenate([state_pool[e][:, 1:], zuv_s[:, None, :POOL_WIDTH]], axis=1))
            sgv_s.append(zuv_s[:, None, POOL_WIDTH + SG_WIDTH:])
            hs = _mm_res([yab_s], [w_out], hs)
        else:
            w = w_in_cd[e]
            wq = w[:, :CD_Q].reshape(D_MODEL, DIFF_KV_HEADS, 2, DIFF_GROUP, DIFF_DH)
            wq = jnp.transpose(wq, (0, 1, 3, 2, 4)).reshape(D_MODEL, CD_Q)
            half = MLA_ROPE // 2
            w_pe = w[:, -MLA_ROPE:]
            w_in = jnp.concatenate([wq, w[:, CD_Q:], w_pe[:, half:], w_pe[:, :half]], axis=1).astype(BF16)
            wqb = mla_w_qb[e].reshape(MLA_Q_LORA, MLA_HEADS, MLA_NOPE + MLA_ROPE)
            wq_pe = wqb[..., MLA_NOPE:]
            wq_pe = jnp.concatenate([wq_pe, wq_pe[..., half:], wq_pe[..., :half]], axis=-1)
            wqb = jnp.concatenate([wqb[..., :MLA_NOPE].reshape(MLA_Q_LORA, -1),
                                   wq_pe.reshape(MLA_Q_LORA, -1)], axis=1).astype(BF16)
            wkvb = mla_w_kvb[e].astype(BF16)
            wkvb_h = mla_w_kvb[e].reshape(MLA_KV_LORA, MLA_HEADS, MLA_NOPE + MLA_VD)
            w_uk_t = jnp.transpose(wkvb_h[..., :MLA_NOPE], (1, 2, 0)).astype(BF16)
            w_uv = jnp.transpose(wkvb_h[..., MLA_NOPE:], (1, 0, 2)).astype(BF16)
            w_out = w_out_cd[e].astype(BF16)
            w_out1, w_out2 = w_out[:DIFF_HEADS * DIFF_VD], w_out[DIFF_HEADS * DIFF_VD:]
            qn, kvn = _row2(mla_q_norm[e]), _row2(mla_kv_norm[e])
            lam_p, subln = diff_lambda[e], _row2(diff_subln[e])

            tb_d, rb_d = _diff_tiles(seq)
            tb_m, rb_m = _mla_tiles(seq)
            dist = (jnp.arange(rb_d)[:, None] - jnp.arange(rb_d + REL_MAX_DIST)[None, :]) + REL_MAX_DIST
            n_keys = past_len + LANES
            kpos = jnp.arange(n_keys)
            dist_s = jnp.where(kpos < past_len, past_len - kpos, 0)
            bias_near, bias_s = _bias_tiles(
                rel_table, dist, _rel_bucket(jnp.maximum(dist, 0)),
                jnp.broadcast_to(_rel_bucket(dist_s)[None], (8, n_keys)))

            cs = _rope_table(jnp.arange(seq))
            (qd, dk, dv, dkb, dvx, ckv, kpe, qm, km, vmx) = _cd_in(
                hp, g_mix, w_in, qn, wqb, kvn, wkvb, cs, batch, seq, False)
            yd = _diff_attn(qd, dkb, dvx, bias_near, lam_p, subln, batch, seq, tb_d, rb_d, li)
            ym = _mla_attn(qm, km, vmx, batch, seq, tb_m, rb_m)
            hp = _mm_res([yd, ym], [w_out1, w_out2], hp)
            rows_p.append((dk.reshape(batch, seq, DIFF_KV_HEADS, 2 * DIFF_DH),
                           dv.reshape(batch, seq, DIFF_KV_HEADS, DIFF_VD),
                           ckv.reshape(batch, seq, MLA_KV_LORA), kpe.reshape(batch, seq, MLA_ROPE)))

            cs_s = jnp.broadcast_to(_rope_table(jnp.full((1,), past_len)), (nb, LANES))
            (qd_s, dk_s, dv_s, ckv_s, kpe_s, qlat, qpe) = _cd_in(
                hs, g_mix, w_in, qn, wqb, kvn, w_uk_t, cs_s, nb, 1, True)
            row_id = jnp.arange(DIFF_MAPS)
            qd_r = jnp.transpose(qd_s.reshape(nb, DIFF_KV_HEADS, DIFF_GROUP, 2, DIFF_DH), (0, 1, 3, 2, 4))
            qd_r = qd_r.reshape(nb, DIFF_MAPS, 1, DIFF_DH)
            slot = ((row_id // DIFF_GROUP) % 2)[:, None] == jnp.arange(2)[None, :]
            qd_pad = jnp.where(slot[None, :, :, None], qd_r, 0.0).reshape(nb, DIFF_MAPS, LANES).astype(BF16)
            pad = ((0, 0), (0, DIFF_MAPS - MLA_HEADS), (0, 0))
            ql = jnp.pad(qlat.reshape(nb, MLA_HEADS, MLA_KV_LORA), pad)
            qp = jnp.pad(qpe.reshape(nb, MLA_HEADS, LANES)[:, :, :MLA_ROPE], pad).astype(BF16)
            row_head = row_id // (2 * DIFF_GROUP)
            bias_rows = bias_s[row_head * DIFF_GROUP + row_id % DIFF_GROUP, 0]
            own = row_head[:, None] == (jnp.arange(past_len * DIFF_KV_HEADS) % DIFF_KV_HEADS)[None, :]
            bias_past = jnp.where(own, jnp.repeat(bias_rows[:, :past_len], DIFF_KV_HEADS, axis=1), NEG_INF)
            od, om = _decode(
                page_table.reshape(-1) * n_odd + e, qd_pad, ql, qp, bias_past, bias_rows[:, past_len:],
                dk_s[:, None], dv_s[:, None], ckv_s[:, None], kpe_s[:, None],
                cache_diff_k.reshape(-1, LANES), cache_diff_v.reshape(-1, LANES),
                cache_mla_ckv.reshape(-1, PAGE_SIZE, MLA_KV_LORA),
                jnp.swapaxes(cache_mla_kpe, 2, 3).reshape(-1, MLA_ROPE, PAGE_SIZE), n_pages)
            od = od.reshape(nb, DIFF_KV_HEADS, 2, DIFF_GROUP * DIFF_VD)
            o_map = [od[:, :, mp].reshape(nb, -1) for mp in range(2)]
            hs = _sample_out(o_map[0], o_map[1], om[:, :MLA_HEADS].reshape(nb, -1), lam_p, subln, w_uv,
                             w_out1, w_out2, hs, li)
            rows_s.append((dk_s.reshape(nb, 1, DIFF_KV_HEADS, 2 * DIFF_DH),
                           dv_s.reshape(nb, 1, DIFF_KV_HEADS, DIFF_VD),
                           ckv_s.reshape(nb, 1, MLA_KV_LORA), kpe_s.reshape(nb, 1, MLA_ROPE)))
        last = li == depth - 1
        hp = _ffn(hp, g_ffn, wu_all, wd_all, li, gf, last)
        hs = _ffn(hs, g_ffn, wu_all, wd_all, li, gf, last)

    y_prompt = hp.reshape(batch, seq, D_MODEL)
    y_sample = hs.reshape(nb, 1, D_MODEL)
    stack = lambda rows, i: jnp.stack([r[i] for r in rows], axis=1)
    return (y_prompt, y_sample, jnp.stack(pool_p, axis=0), jnp.stack(pool_s, axis=0), jnp.stack(sgv_s, axis=0),
            stack(rows_p, 0), stack(rows_p, 1), stack(rows_p, 2), stack(rows_p, 3),
            stack(rows_s, 0), stack(rows_s, 1), stack(rows_s, 2), stack(rows_s, 3))
```

```python
import functools
import math

import jax
import jax.numpy as jnp
from jax import lax
from jax.experimental import pallas as pl
from jax.experimental.pallas import tpu as pltpu

D_MODEL = 2048
PAGE_SIZE = 128
POOL_WINDOWS = (2, 4, 8, 16)
POOL_GROUPS = len(POOL_WINDOWS)
POOL_WIDTH = D_MODEL // 2
POOL_GW = POOL_WIDTH // POOL_GROUPS
POOL_BUF = max(POOL_WINDOWS) - 1
POOL_HIST = 16
POOL_PAD = 8
assert POOL_WINDOWS == tuple(2 ** (k + 1) for k in range(POOL_GROUPS)) and POOL_PAD >= POOL_WINDOWS[-1] // 2
SG_CHUNK = 128
SG_GROUPS = 8
SG_WIDTH = D_MODEL // 2
SG_GW = SG_WIDTH // SG_GROUPS
DIFF_HEADS = 8
DIFF_KV_HEADS = 2
DIFF_GROUP = DIFF_HEADS // DIFF_KV_HEADS
DIFF_DH = 64
DIFF_VD = 2 * DIFF_DH
DIFF_SCALE = DIFF_DH ** -0.5
DIFF_MAPS = 2 * DIFF_HEADS
MLA_HEADS = 8
MLA_Q_LORA = 512
MLA_KV_LORA = 512
MLA_NOPE = 128
MLA_ROPE = 64
MLA_VD = 128
MLA_SCALE = (MLA_NOPE + MLA_ROPE) ** -0.5
ROPE_BASE = 10000.0
REL_BUCKETS = 32
REL_MAX_DIST = 128
D_FF = 4 * D_MODEL
EPS = 1e-6
NEG_INF = -1e30

LANES = 128
CD_Q = DIFF_HEADS * 2 * DIFF_DH
CD_K = DIFF_KV_HEADS * 2 * DIFF_DH
CD_V = DIFF_KV_HEADS * DIFF_VD
CD_IN_PAD = CD_Q + CD_K + CD_V + MLA_Q_LORA + MLA_KV_LORA + 2 * MLA_ROPE
MLA_QK = 2 * LANES
VX = 2 * LANES
LOG2E = math.log2(math.e)
MLA_HEADS_PER_STEP = 4
DECODE_SLOTS = 4

BF16 = jnp.bfloat16
F32 = jnp.float32
VMEM_LIMIT_MB = 56


def _rms(x, g):
    return x * lax.rsqrt(jnp.mean(x * x, axis=-1, keepdims=True) + EPS) * g


def _dot(a, b):
    return jnp.dot(a, b, preferred_element_type=F32)


def _dot_nt(a, b):
    return lax.dot_general(a, b, (((1,), (1,)), ((), ())), preferred_element_type=F32)


def _params(n_axes):
    return pltpu.CompilerParams(dimension_semantics=("arbitrary",) * n_axes,
                                vmem_limit_bytes=VMEM_LIMIT_MB * 2 ** 20)


def _full(shape):
    zeros = (0,) * len(shape)
    return pl.BlockSpec(shape, lambda *_: zeros)


def _ffn_kernel(h_ref, g_ref, wu_ref, wd_ref, gf_ref, o_ref, xn_ref, *, final_norm):
    f = pl.program_id(1)

    @pl.when(f == 0)
    def _():
        h = h_ref[...]
        xn_ref[...] = _rms(h, g_ref[...]).astype(BF16)
        o_ref[...] = h

    a = jnp.maximum(_dot(xn_ref[...], wu_ref[0]), 0.0)
    o_ref[...] += _dot((a * a).astype(BF16), wd_ref[0])

    if final_norm:
        @pl.when(f == pl.num_programs(1) - 1)
        def _():
            o_ref[...] = _rms(o_ref[...], gf_ref[...])


def _ffn(h, g, wu, wd, li, gf, final_norm):
    m = h.shape[0]
    tm, tf = min(m, 512), 1024
    return pl.pallas_call(
        functools.partial(_ffn_kernel, final_norm=final_norm),
        grid=(m // tm, D_FF // tf),
        in_specs=[pl.BlockSpec((tm, D_MODEL), lambda i, f: (i, 0)),
                  _full((1, D_MODEL)),
                  pl.BlockSpec((1, D_MODEL, tf), lambda i, f: (li, 0, f)),
                  pl.BlockSpec((1, tf, D_MODEL), lambda i, f: (li, f, 0)),
                  _full((1, D_MODEL))],
        out_specs=pl.BlockSpec((tm, D_MODEL), lambda i, f: (i, 0)),
        out_shape=jax.ShapeDtypeStruct((m, D_MODEL), F32),
        scratch_shapes=[pltpu.VMEM((tm, D_MODEL), BF16)],
        compiler_params=_params(2), name="ffn")(h, g, wu, wd, gf)


def _mm_res_kernel(*refs, n):
    acc = refs[2 * n][...]
    for y_ref, w_ref in zip(refs[:n], refs[n:2 * n]):
        acc = acc + _dot(y_ref[...], w_ref[...])
    refs[2 * n + 1][...] = acc


def _mm_res(ys, ws, res):
    m = res.shape[0]
    tm = min(m, 512)
    n = len(ys)
    return pl.pallas_call(
        functools.partial(_mm_res_kernel, n=n),
        grid=(m // tm,),
        in_specs=([pl.BlockSpec((tm, y.shape[1]), lambda i: (i, 0)) for y in ys]
                  + [_full(w.shape) for w in ws]
                  + [pl.BlockSpec((tm, D_MODEL), lambda i: (i, 0))]),
        out_specs=pl.BlockSpec((tm, D_MODEL), lambda i: (i, 0)),
        out_shape=jax.ShapeDtypeStruct((m, D_MODEL), F32),
        compiler_params=_params(1), name="mm_res")(*ys, *ws, res)


def _ab_in_kernel(h_ref, g_ref, w_ref, sgn_ref, o_ref):
    xn = _rms(h_ref[...], g_ref[...]).astype(BF16)
    z_hi, u_hi = POOL_WIDTH, POOL_WIDTH + SG_WIDTH
    o_ref[:, :z_hi] = _dot(xn, w_ref[:, :z_hi])
    o_ref[:, z_hi:u_hi] = jax.nn.gelu(_dot(xn, w_ref[:, z_hi:u_hi]))
    o_ref[:, u_hi:] = _rms(jax.nn.gelu(_dot(xn, w_ref[:, u_hi:])), sgn_ref[...])


def _ab_in(h, g, w, sgn):
    m = h.shape[0]
    tm = min(m, 256)
    width = w.shape[1]
    return pl.pallas_call(
        _ab_in_kernel,
        grid=(m // tm,),
        in_specs=[pl.BlockSpec((tm, D_MODEL), lambda i: (i, 0)), _full((1, D_MODEL)), _full(w.shape),
                  _full((1, SG_WIDTH))],
        out_specs=pl.BlockSpec((tm, width), lambda i: (i, 0)),
        out_shape=jax.ShapeDtypeStruct((m, width), F32),
        compiler_params=_params(1), name="ab_in")(h, g, w, sgn)


def _ab_mix_kernel(z_ref, u_ref, v_ref, pw_ref, ps_ref, sw_ref, sbt_ref, o_ref, zbuf_ref, sa_ref, sb_ref, *, tm):
    t = pl.program_id(1)
    base = POOL_PAD + POOL_HIST
    n = base + tm

    @pl.when(t == 0)
    def _():
        zbuf_ref[0:base, :] = jnp.zeros((base, POOL_WIDTH), F32)
        sa_ref[0:POOL_PAD, :] = jnp.zeros((POOL_PAD, POOL_WIDTH), F32)
        sb_ref[0:POOL_PAD, :] = jnp.zeros((POOL_PAD, POOL_WIDTH), F32)

    z = z_ref[...]
    zbuf_ref[base:n, :] = z
    src, dst = zbuf_ref, sa_ref
    sums = []
    for level, w in enumerate(POOL_WINDOWS):
        lo, shift = level * POOL_GW, w // 2
        dst[POOL_PAD:n, lo:] = src[POOL_PAD:n, lo:] + src[POOL_PAD - shift:n - shift, lo:]
        sums.append(dst)
        src, dst = dst, (sb_ref if dst is sa_ref else sa_ref)
    pos = t * tm + lax.broadcasted_iota(jnp.int32, (tm, 1), 0)
    for gi, w in enumerate(POOL_WINDOWS):
        lo, hi = gi * POOL_GW, (gi + 1) * POOL_GW
        cnt = jnp.minimum(pos + 1, w).astype(F32)
        pooled = sums[gi][base:n, lo:hi] / cnt - z[:, lo:hi]
        ya = _dot(pooled.astype(BF16), pw_ref[gi]) * ps_ref[:, lo:hi]
        o_ref[:, lo:hi] = ya.astype(BF16)
    zbuf_ref[POOL_PAD:base, :] = zbuf_ref[n - POOL_HIST:n, :]

    row = lax.broadcasted_iota(jnp.int32, (SG_CHUNK, SG_CHUNK), 0)
    col = lax.broadcasted_iota(jnp.int32, (SG_CHUNK, SG_CHUNK), 1)
    for g in range(SG_GROUPS):
        wg = jnp.where(row >= col, sw_ref[g], 0.0).astype(BF16)
        bg = sbt_ref[:, g:g + 1]
        lo, hi = g * SG_GW, (g + 1) * SG_GW
        for c in range(tm // SG_CHUNK):
            r0, r1 = c * SG_CHUNK, (c + 1) * SG_CHUNK
            mixed = _dot(wg, v_ref[r0:r1, lo:hi].astype(BF16)) + bg
            o_ref[r0:r1, POOL_WIDTH + lo:POOL_WIDTH + hi] = (u_ref[r0:r1, lo:hi] * mixed).astype(BF16)


def _ab_mix(zuv, batch, seq, pool_w, pool_scale, sg_w, sg_bt):
    tm = min(seq, 512)
    nt = seq // tm
    col = lambda c: pl.BlockSpec((tm, POOL_WIDTH), lambda b, t: (b * nt + t, c))
    return pl.pallas_call(
        functools.partial(_ab_mix_kernel, tm=tm),
        grid=(batch, nt),
        in_specs=[col(0), col(1), col(2),
                  _full(pool_w.shape), _full(pool_scale.shape), _full(sg_w.shape), _full(sg_bt.shape)],
        out_specs=pl.BlockSpec((tm, D_MODEL), lambda b, t: (b * nt + t, 0)),
        out_shape=jax.ShapeDtypeStruct((batch * seq, D_MODEL), BF16),
        scratch_shapes=[pltpu.VMEM((POOL_PAD + POOL_HIST + tm, POOL_WIDTH), F32)] * 3,
        compiler_params=_params(2), name="ab_mix")(zuv, zuv, zuv, pool_w, pool_scale, sg_w, sg_bt)


def _ab_mix_s_kernel(zuv_ref, st_ref, pw_ref, ps_ref, w0_ref, b0_ref, o_ref, *, cnts):
    z = zuv_ref[:, 0:POOL_WIDTH]
    for gi, w in enumerate(POOL_WINDOWS):
        lo, hi = gi * POOL_GW, (gi + 1) * POOL_GW
        s = z[:, lo:hi]
        for i in range(1, w):
            s = s + st_ref[POOL_BUF - i, :, lo:hi]
        pooled = s / cnts[gi] - z[:, lo:hi]
        ya = _dot(pooled.astype(BF16), pw_ref[gi]) * ps_ref[:, lo:hi]
        o_ref[:, lo:hi] = ya.astype(BF16)
    u = zuv_ref[:, POOL_WIDTH:POOL_WIDTH + SG_WIDTH]
    v = zuv_ref[:, POOL_WIDTH + SG_WIDTH:]
    o_ref[:, POOL_WIDTH:] = (u * (w0_ref[...] * v + b0_ref[...])).astype(BF16)


def _ab_mix_s(zuv, state, pool_w, pool_scale, w0, b0, cnts):
    m = zuv.shape[0]
    return pl.pallas_call(
        functools.partial(_ab_mix_s_kernel, cnts=cnts),
        grid=(1,),
        in_specs=[_full(zuv.shape), _full(state.shape), _full(pool_w.shape), _full(pool_scale.shape),
                  _full(w0.shape), _full(b0.shape)],
        out_specs=_full((m, D_MODEL)),
        out_shape=jax.ShapeDtypeStruct((m, D_MODEL), BF16),
        compiler_params=_params(1), name="ab_mix_s")(zuv, state, pool_w, pool_scale, w0, b0)


def _rope128(t, cs):
    u = t * cs
    return u + pltpu.roll(u, 2 * 32, axis=1)


def _cd_in_kernel(h_ref, g_ref, win_ref, qn_ref, wqb_ref, kvn_ref, wkv_ref, cs_ref, *outs, sample):
    xn = _rms(h_ref[...], g_ref[...]).astype(BF16)
    p = _dot(xn, win_ref[...])
    tm = p.shape[0]
    lane = lax.broadcasted_iota(jnp.int32, (tm, LANES), 1)
    first = lane < MLA_ROPE
    cs = cs_ref[...]
    o_k, o_v = CD_Q, CD_Q + CD_K
    o_cq, o_ckv, o_pe = o_v + CD_V, o_v + CD_V + MLA_Q_LORA, o_v + CD_V + MLA_Q_LORA + MLA_KV_LORA
    dk, dv = p[:, o_k:o_v], p[:, o_v:o_cq]
    cqn = _rms(p[:, o_cq:o_ckv], qn_ref[...]).astype(BF16)
    ckv = _rms(p[:, o_ckv:o_pe], kvn_ref[...])
    rk = _rope128(p[:, o_pe:o_pe + LANES], cs)
    q = _dot(cqn, wqb_ref[...])
    q_nope = MLA_HEADS * MLA_NOPE
    if sample:
        qd_ref, dk_ref, dv_ref, ckv_ref, kpe_ref, qlat_ref, qpe_ref = outs
        qd_ref[...] = p[:, 0:CD_Q] * DIFF_SCALE
        for h in range(MLA_HEADS):
            qn = q[:, h * MLA_NOPE:(h + 1) * MLA_NOPE].astype(BF16)
            qlat_ref[:, h * MLA_KV_LORA:(h + 1) * MLA_KV_LORA] = _dot(qn, wkv_ref[h]).astype(BF16)
            qpe_ref[:, h * LANES:(h + 1) * LANES] = _rope128(
                q[:, q_nope + h * LANES:q_nope + (h + 1) * LANES], cs)
    else:
        qd_ref, dk_ref, dv_ref, dkb_ref, dvx_ref, ckv_ref, kpe_ref, qm_ref, km_ref, vmx_ref = outs
        ones = jnp.ones((tm, LANES), BF16)
        for hg in range(DIFF_HEADS):
            slot = p[:, hg * LANES:(hg + 1) * LANES] * (DIFF_SCALE * LOG2E)
            h, g = divmod(hg, DIFF_GROUP)
            qd_ref[0, (h * 2 + 0) * DIFF_GROUP + g] = jnp.where(first, slot, 0.0).astype(BF16)
            qd_ref[0, (h * 2 + 1) * DIFF_GROUP + g] = jnp.where(first, 0.0, slot).astype(BF16)
        dkb_ref[...] = dk.astype(BF16)
        for h in range(DIFF_KV_HEADS):
            dvx_ref[:, h * VX:h * VX + DIFF_VD] = dv[:, h * DIFF_VD:(h + 1) * DIFF_VD].astype(BF16)
            dvx_ref[:, h * VX + DIFF_VD:(h + 1) * VX] = ones
        kv = _dot(ckv.astype(BF16), wkv_ref[...])
        kpe_slot = jnp.where(first, rk, 0.0).astype(BF16)
        q = q * (MLA_SCALE * LOG2E)
        for h in range(MLA_HEADS):
            qm_ref[:, h * MLA_QK:h * MLA_QK + LANES] = q[:, h * MLA_NOPE:(h + 1) * MLA_NOPE].astype(BF16)
            qm_ref[:, h * MLA_QK + LANES:(h + 1) * MLA_QK] = _rope128(
                q[:, q_nope + h * LANES:q_nope + (h + 1) * LANES], cs).astype(BF16)
            km_ref[:, h * MLA_QK:h * MLA_QK + LANES] = kv[:, h * 2 * LANES:h * 2 * LANES + LANES].astype(BF16)
            km_ref[:, h * MLA_QK + LANES:(h + 1) * MLA_QK] = kpe_slot
            vmx_ref[:, h * VX:h * VX + MLA_VD] = kv[:, h * 2 * LANES + LANES:(h + 1) * 2 * LANES].astype(BF16)
            vmx_ref[:, h * VX + MLA_VD:(h + 1) * VX] = ones
    dk_ref[...] = dk
    dv_ref[...] = dv
    ckv_ref[...] = ckv
    kpe_ref[...] = rk[:, 0:MLA_ROPE]


def _cd_in(h, g, win, qn, wqb, kvn, wkv, cs, batch, seq, sample):
    m = h.shape[0]
    tm = min(seq, 256) if not sample else m
    nt = seq // tm if not sample else 1
    row = lambda width, dt: (pl.BlockSpec((tm, width), lambda i: (i, 0)), jax.ShapeDtypeStruct((m, width), dt))
    common = [row(CD_K, F32), row(CD_V, F32)]
    tail = [row(MLA_KV_LORA, F32), row(MLA_ROPE, F32)]
    if sample:
        outs = ([row(CD_Q, F32)] + common + tail
                + [row(MLA_HEADS * MLA_KV_LORA, BF16), row(MLA_HEADS * LANES, F32)])
        cs_spec = _full(cs.shape)
    else:
        qd = (pl.BlockSpec((1, DIFF_MAPS, tm, LANES), lambda i: (i // nt, 0, i % nt, 0)),
              jax.ShapeDtypeStruct((batch, DIFF_MAPS, seq, LANES), BF16))
        outs = ([qd] + common + [row(CD_K, BF16), row(DIFF_KV_HEADS * VX, BF16)] + tail
                + [row(MLA_HEADS * MLA_QK, BF16), row(MLA_HEADS * MLA_QK, BF16), row(MLA_HEADS * VX, BF16)])
        cs_spec = pl.BlockSpec((tm, LANES), lambda i: (i % nt, 0))
    return pl.pallas_call(
        functools.partial(_cd_in_kernel, sample=sample),
        grid=(m // tm,),
        in_specs=[pl.BlockSpec((tm, D_MODEL), lambda i: (i, 0)), _full(g.shape), _full(win.shape),
                  _full(qn.shape), _full(wqb.shape), _full(kvn.shape), _full(wkv.shape), cs_spec],
        out_specs=[o[0] for o in outs],
        out_shape=[o[1] for o in outs],
        compiler_params=_params(1), name="cd_in_s" if sample else "cd_in")(h, g, win, qn, wqb, kvn, wkv, cs)


def _bias_kernel(tab_ref, dist_ref, bn_ref, bs_ref, on_ref, os_ref):
    def lookup(bkt, head):
        val = jnp.zeros(bkt.shape, F32)
        for b in range(REL_BUCKETS):
            val = jnp.where(bkt == b, tab_ref[b, head], val)
        return val

    future = dist_ref[...] < 0
    bn, bs = bn_ref[...], bs_ref[...]
    for head in range(DIFF_HEADS):
        far = tab_ref[REL_BUCKETS - 1, head]
        on_ref[head] = jnp.where(future, NEG_INF, (lookup(bn, head) - far) * LOG2E)
        os_ref[head] = lookup(bs, head)


def _bias_tiles(rel_table, dist_near, bkt_near, bkt_sample):
    vmem = pl.BlockSpec(memory_space=pltpu.VMEM)
    return pl.pallas_call(
        _bias_kernel,
        in_specs=[pl.BlockSpec(memory_space=pltpu.SMEM), vmem, vmem, vmem],
        out_specs=[vmem, vmem],
        out_shape=[jax.ShapeDtypeStruct((DIFF_HEADS,) + bkt_near.shape, F32),
                   jax.ShapeDtypeStruct((DIFF_HEADS,) + bkt_sample.shape, F32)],
        name="rel_bias")(rel_table, dist_near, bkt_near, bkt_sample)


def _attn_block(q, k, vx, m_ref, acc_ref, row0, near=None):
    rb, c = q.shape[0], k.shape[0]
    s = _dot_nt(q, k)
    if near is not None:
        s = near(s)
    rows = pl.ds(row0, rb)
    m_prev = m_ref[rows, :]
    m_new = jnp.maximum(m_prev, jnp.max(s, axis=-1, keepdims=True))
    alpha = jnp.exp2(m_prev - m_new)
    p = jnp.exp2(s - jnp.concatenate([m_new] * (c // LANES), axis=1)).astype(BF16)
    acc_ref[rows, :] = jnp.concatenate([alpha] * (VX // LANES), axis=1) * acc_ref[rows, :] + _dot(p, vx)
    m_ref[rows, :] = m_new


def _attn_init(m_ref, acc_ref):
    m_ref[...] = jnp.full(m_ref.shape, NEG_INF, F32)
    acc_ref[...] = jnp.zeros(acc_ref.shape, F32)


def _causal_pairs(n):
    pairs = [(i, j) for i in range(n) for j in range(i + 1)]
    return (jnp.asarray([p[0] for p in pairs], jnp.int32), jnp.asarray([p[1] for p in pairs], jnp.int32))


def _softmax_init(m_ref, l_ref, acc_ref):
    m_ref[...] = jnp.full(m_ref.shape, NEG_INF, F32)
    l_ref[...] = jnp.zeros(l_ref.shape, F32)
    acc_ref[...] = jnp.zeros(acc_ref.shape, F32)


def _diff_lambda(lp, li):
    lam_init = 0.8 - 0.6 * math.exp(-0.3 * li)
    a = jnp.sum(lp[0:1, :] * lp[1:2, :], axis=-1, keepdims=True)
    b = jnp.sum(lp[2:3, :] * lp[3:4, :], axis=-1, keepdims=True)
    return jnp.exp(a) - jnp.exp(b) + lam_init, lam_init


def _diff_attn_kernel(ii_ref, jj_ref, q_ref, k_ref, vx_ref, near_ref, lam_ref, sub_ref, o_ref,
                      m_ref, acc_ref, *, tb, rb, li):
    t = pl.program_id(2)
    i, j = ii_ref[t], jj_ref[t]
    slots = 2 * DIFF_GROUP
    nrb = tb // rb
    band = REL_MAX_DIST

    @pl.when(j == 0)
    def _():
        _attn_init(m_ref, acc_ref)

    def tile(kind):
        k, vx = k_ref[...], vx_ref[...]
        for slot in range(slots):
            g = slot % DIFF_GROUP
            for b in range(nrb):
                r0 = b * rb
                q = q_ref[0, slot, r0:r0 + rb, :]
                kk, vv, near = k, vx, None
                if kind == "prev" and b == 0:
                    near = lambda s, g=g: jnp.concatenate(
                        [s[:, :tb - band], s[:, tb - band:] + near_ref[g, :, 0:band]], axis=1)
                elif kind == "diag":
                    kk, vv = k[:r0 + rb], vx[:r0 + rb]
                    if b == 0:
                        near = lambda s, g=g: s + near_ref[g, :, band:]
                    elif r0 == band:
                        near = lambda s, g=g: s + near_ref[g]
                    else:
                        near = lambda s, g=g, r0=r0: jnp.concatenate(
                            [s[:, :r0 - band], s[:, r0 - band:] + near_ref[g]], axis=1)
                _attn_block(q, kk, vv, m_ref, acc_ref, slot * tb + r0, near)

    @pl.when(j < i - 1)
    def _():
        tile("far")

    @pl.when(j == i - 1)
    def _():
        tile("prev")

    @pl.when(j == i)
    def _():
        tile("diag")
        lam, lam_init = _diff_lambda(lam_ref[...], li)
        for g in range(DIFF_GROUP):
            a0 = acc_ref[g * tb:(g + 1) * tb, :]
            a1 = acc_ref[(DIFF_GROUP + g) * tb:(DIFF_GROUP + g + 1) * tb, :]
            y = a0[:, :DIFF_VD] / a0[:, DIFF_VD:] - lam * (a1[:, :DIFF_VD] / a1[:, DIFF_VD:])
            o_ref[:, g * DIFF_VD:(g + 1) * DIFF_VD] = (_rms(y, sub_ref[...]) * (1.0 - lam_init)).astype(BF16)


def _diff_attn(qd, kb, vx, near, lam_p, subln, batch, seq, tb, rb, li):
    n = seq // tb
    slots = 2 * DIFF_GROUP
    ii, jj = _causal_pairs(n)
    grid_spec = pltpu.PrefetchScalarGridSpec(
        num_scalar_prefetch=2, grid=(batch, DIFF_KV_HEADS, ii.shape[0]),
        in_specs=[pl.BlockSpec((1, slots, tb, LANES), lambda b, h, t, ii, jj: (b, h, ii[t], 0)),
                  pl.BlockSpec((tb, LANES), lambda b, h, t, ii, jj: (b * n + jj[t], h)),
                  pl.BlockSpec((tb, VX), lambda b, h, t, ii, jj: (b * n + jj[t], h)),
                  pl.BlockSpec((DIFF_GROUP,) + near.shape[1:], lambda b, h, t, ii, jj: (h, 0, 0)),
                  pl.BlockSpec(lam_p.shape, lambda b, h, t, ii, jj: (0, 0)),
                  pl.BlockSpec(subln.shape, lambda b, h, t, ii, jj: (0, 0))],
        out_specs=pl.BlockSpec((tb, DIFF_GROUP * DIFF_VD), lambda b, h, t, ii, jj: (b * n + ii[t], h)),
        scratch_shapes=[pltpu.VMEM((slots * tb, LANES), F32), pltpu.VMEM((slots * tb, VX), F32)])
    return pl.pallas_call(
        functools.partial(_diff_attn_kernel, tb=tb, rb=rb, li=li),
        grid_spec=grid_spec,
        out_shape=jax.ShapeDtypeStruct((batch * seq, DIFF_HEADS * DIFF_VD), BF16),
        compiler_params=_params(3), name="diff_attn")(ii, jj, qd, kb, vx, near, lam_p, subln)


def _mla_attn_kernel(ii_ref, jj_ref, q_ref, k_ref, vx_ref, o_ref, m_ref, acc_ref, *, tb, rb, hps):
    t = pl.program_id(2)
    i, j = ii_ref[t], jj_ref[t]

    @pl.when(j == 0)
    def _():
        _attn_init(m_ref, acc_ref)

    def tile(diag):
        causal = (lax.broadcasted_iota(jnp.int32, (rb, rb), 0) >= lax.broadcasted_iota(jnp.int32, (rb, rb), 1))
        for h in range(hps):
            k = k_ref[:, h * MLA_QK:(h + 1) * MLA_QK]
            vx = vx_ref[:, h * VX:(h + 1) * VX]
            for b in range(tb // rb):
                r0 = b * rb
                q = q_ref[r0:r0 + rb, h * MLA_QK:(h + 1) * MLA_QK]
                row0 = h * tb + r0
                if not diag:
                    _attn_block(q, k, vx, m_ref, acc_ref, row0)
                elif b == 0:
                    _attn_block(q, k[:rb], vx[:rb], m_ref, acc_ref, row0,
                                lambda s: jnp.where(causal, s, NEG_INF))
                else:
                    near = lambda s, r0=r0: jnp.concatenate(
                        [s[:, :r0], jnp.where(causal, s[:, r0:], NEG_INF)], axis=1)
                    _attn_block(q, k[:r0 + rb], vx[:r0 + rb], m_ref, acc_ref, row0, near)

    @pl.when(j < i)
    def _():
        tile(False)

    @pl.when(j == i)
    def _():
        tile(True)
        for h in range(hps):
            acc = acc_ref[h * tb:(h + 1) * tb, :]
            o_ref[:, h * MLA_VD:(h + 1) * MLA_VD] = (acc[:, :MLA_VD] / acc[:, MLA_VD:]).astype(BF16)


def _mla_attn(qm, km, vmx, batch, seq, tb, rb):
    n = seq // tb
    hps = MLA_HEADS_PER_STEP
    ii, jj = _causal_pairs(n)
    grid_spec = pltpu.PrefetchScalarGridSpec(
        num_scalar_prefetch=2, grid=(batch, MLA_HEADS // hps, ii.shape[0]),
        in_specs=[pl.BlockSpec((tb, hps * MLA_QK), lambda b, h, t, ii, jj: (b * n + ii[t], h)),
                  pl.BlockSpec((tb, hps * MLA_QK), lambda b, h, t, ii, jj: (b * n + jj[t], h)),
                  pl.BlockSpec((tb, hps * VX), lambda b, h, t, ii, jj: (b * n + jj[t], h))],
        out_specs=pl.BlockSpec((tb, hps * MLA_VD), lambda b, h, t, ii, jj: (b * n + ii[t], h)),
        scratch_shapes=[pltpu.VMEM((hps * tb, LANES), F32), pltpu.VMEM((hps * tb, VX), F32)])
    return pl.pallas_call(
        functools.partial(_mla_attn_kernel, tb=tb, rb=rb, hps=hps),
        grid_spec=grid_spec,
        out_shape=jax.ShapeDtypeStruct((batch * seq, MLA_HEADS * MLA_VD), BF16),
        compiler_params=_params(3), name="mla_attn")(ii, jj, qm, km, vmx)


def _decode_kernel(pt_ref, qd_ref, ql_ref, qp_ref, bias_ref, bnew_ref, kdn_ref, vdn_ref, ckvn_ref, kpen_ref,
                   cdk_ref, cdv_ref, cckv_ref, ckpe_ref, od_ref, om_ref, *rest, pages, nch):
    nslot = DECODE_SLOTS
    bufs = tuple(rest[4 * i:4 * i + 4] for i in range(nslot))
    sem, md_ref, ld_ref, accd_ref, mm_ref, lm_ref, accm_ref = rest[4 * nslot:]
    s = pl.program_id(0)
    last = pl.num_programs(0) - 1
    total = pl.num_programs(0) * nslot
    c = lax.rem(s, nch)
    drows = PAGE_SIZE * DIFF_KV_HEADS

    def page_copies(chunk, slot):
        out = []
        for k in range(pages):
            pg = 0 if chunk is None else pt_ref[chunk * pages + k]
            row = pl.multiple_of(pg * drows, drows)
            srcs = (cdk_ref.at[pl.ds(row, drows), :], cdv_ref.at[pl.ds(row, drows), :],
                    cckv_ref.at[pg], ckpe_ref.at[pg])
            for a, src in enumerate(srcs):
                out.append(pltpu.make_async_copy(src, bufs[slot][a].at[k], sem.at[slot, a]))
        return out

    def start(chunk, slot):
        for n, cp in enumerate(page_copies(chunk, slot)):
            cp.start(priority=(n // 4) % 2)

    def wait(slot):
        for cp in page_copies(None, slot):
            cp.wait()

    @pl.when(s == 0)
    def _():
        for j in range(nslot - 1):
            start(j, j)

    @pl.when(c == 0)
    def _():
        _softmax_init(md_ref, ld_ref, accd_ref)
        _softmax_init(mm_ref, lm_ref, accm_ref)

    qd, ql, qp = qd_ref[0], ql_ref[0], qp_ref[0]

    def update(s, vals, m_ref, l_ref, acc_ref):
        m_prev = m_ref[...]
        m_new = jnp.maximum(m_prev, jnp.max(s, axis=-1, keepdims=True))
        alpha = jnp.exp(m_prev - m_new)
        p = jnp.exp(s - m_new)
        l_ref[...] = alpha * l_ref[...] + jnp.sum(p, axis=-1, keepdims=True)
        pb = p.astype(BF16)
        acc = alpha * acc_ref[...]
        width = vals[0].shape[0]
        for k, val in enumerate(vals):
            acc = acc + _dot(pb[:, k * width:(k + 1) * width], val)
        acc_ref[...] = acc
        m_ref[...] = m_new

    def attend(slot):
        kd, vd, ckv, kpe = bufs[slot]
        vds = [vd[k].astype(BF16) for k in range(pages)]
        ckvs = [ckv[k].astype(BF16) for k in range(pages)]
        sd = (jnp.concatenate([_dot_nt(qd, kd[k].astype(BF16)) for k in range(pages)], axis=1)
              + bias_ref[:, slot * pages * drows:(slot + 1) * pages * drows])
        sm = jnp.concatenate([_dot_nt(ql, ckvs[k]) + _dot(qp, kpe[k].astype(BF16))
                              for k in range(pages)], axis=1) * MLA_SCALE
        update(sd, vds, md_ref, ld_ref, accd_ref)
        update(sm, ckvs, mm_ref, lm_ref, accm_ref)

    for j in range(nslot):
        wait(j)
        start(lax.rem(s * nslot + j + nslot - 1, total), (j + nslot - 1) % nslot)
        attend(j)

    @pl.when(s == last)
    def _():
        for j in range(nslot - 1):
            wait(j)

    @pl.when(c == nch - 1)
    def _():
        def finish(s_new, v_new, m_ref, l_ref, acc_ref, o_ref):
            m_prev = m_ref[...]
            m_new = jnp.maximum(m_prev, s_new)
            alpha = jnp.exp(m_prev - m_new)
            p = jnp.exp(s_new - m_new)
            l = alpha * l_ref[...] + p
            o_ref[0] = (alpha * acc_ref[...] + p * v_new) / l

        head0 = lax.broadcasted_iota(jnp.int32, (DIFF_MAPS, LANES), 0) < 2 * DIFF_GROUP
        kdn, vdn = kdn_ref[0], vdn_ref[0]
        kd_new = jnp.where(head0, kdn[:, 0:LANES], kdn[:, LANES:])
        vd_new = jnp.where(head0, vdn[:, 0:DIFF_VD], vdn[:, DIFF_VD:])
        sd_new = jnp.sum(qd.astype(F32) * kd_new, axis=-1, keepdims=True) + bnew_ref[:, 0:1]
        finish(sd_new, vd_new, md_ref, ld_ref, accd_ref, od_ref)
        sm_new = (jnp.sum(ql.astype(F32) * ckvn_ref[0], axis=-1, keepdims=True)
                  + jnp.sum(qp.astype(F32) * kpen_ref[0], axis=-1, keepdims=True)) * MLA_SCALE
        finish(sm_new, ckvn_ref[0], mm_ref, lm_ref, accm_ref, om_ref)


def _decode(page_idx, qd, ql, qp, bias, bias_new, kd_new, vd_new, ckv_new, kpe_new, c_dk, c_dv, c_ckv, c_kpe,
            n_pages):
    nb = qd.shape[0]
    nslot = DECODE_SLOTS
    pages = next(p for p in (8, 4, 2, 1) if n_pages % (nslot * p) == 0)
    nch = n_pages // (nslot * pages)
    rows = DIFF_MAPS
    drows = PAGE_SIZE * DIFF_KV_HEADS
    per_b = lambda shape: pl.BlockSpec((1,) + shape, lambda s, pt: (s // nch, 0, 0))
    hbm = pl.BlockSpec(memory_space=pl.ANY)
    page_bufs = [pltpu.VMEM((pages, drows, LANES), F32), pltpu.VMEM((pages, drows, LANES), F32),
                 pltpu.VMEM((pages, PAGE_SIZE, MLA_KV_LORA), F32), pltpu.VMEM((pages, MLA_ROPE, PAGE_SIZE), F32)]
    in_specs = [per_b((rows, LANES)), per_b((rows, MLA_KV_LORA)), per_b((rows, MLA_ROPE)),
                pl.BlockSpec((rows, nslot * pages * drows), lambda s, pt: (0, s % nch)),
                pl.BlockSpec(bias_new.shape, lambda s, pt: (0, 0)),
                per_b((1, CD_K)), per_b((1, CD_V)), per_b((1, MLA_KV_LORA)), per_b((1, MLA_ROPE)),
                hbm, hbm, hbm, hbm]
    grid_spec = pltpu.PrefetchScalarGridSpec(
        num_scalar_prefetch=1, grid=(nb * nch,), in_specs=in_specs,
        out_specs=[per_b((rows, DIFF_VD)), per_b((rows, MLA_KV_LORA))],
        scratch_shapes=(page_bufs * nslot + [pltpu.SemaphoreType.DMA((nslot, 4))]
                        + [pltpu.VMEM((rows, 1), F32), pltpu.VMEM((rows, 1), F32), pltpu.VMEM((rows, DIFF_VD), F32),
                           pltpu.VMEM((rows, 1), F32), pltpu.VMEM((rows, 1), F32),
                           pltpu.VMEM((rows, MLA_KV_LORA), F32)]))
    return pl.pallas_call(
        functools.partial(_decode_kernel, pages=pages, nch=nch),
        grid_spec=grid_spec,
        out_shape=[jax.ShapeDtypeStruct((nb, rows, DIFF_VD), F32),
                   jax.ShapeDtypeStruct((nb, rows, MLA_KV_LORA), F32)],
        compiler_params=_params(1), name="decode_attn")(
            page_idx, qd, ql, qp, bias, bias_new, kd_new, vd_new, ckv_new, kpe_new, c_dk, c_dv, c_ckv, c_kpe)


def _sample_out_kernel(o0_ref, o1_ref, om_ref, lam_ref, sub_ref, wuv_ref, w1_ref, w2_ref, res_ref, o_ref, *, li):
    lam, lam_init = _diff_lambda(lam_ref[...], li)
    acc = res_ref[...]
    for hd in range(DIFF_HEADS):
        lo, hi = hd * DIFF_VD, (hd + 1) * DIFF_VD
        y = o0_ref[:, lo:hi] - lam * o1_ref[:, lo:hi]
        yd = (_rms(y, sub_ref[...]) * (1.0 - lam_init)).astype(BF16)
        acc = acc + _dot(yd, w1_ref[lo:hi, :])
    for h in range(MLA_HEADS):
        om = om_ref[:, h * MLA_KV_LORA:(h + 1) * MLA_KV_LORA].astype(BF16)
        ym = _dot(om, wuv_ref[h]).astype(BF16)
        acc = acc + _dot(ym, w2_ref[h * MLA_VD:(h + 1) * MLA_VD, :])
    o_ref[...] = acc


def _sample_out(o0, o1, om, lam_p, subln, wuv, w1, w2, res, li):
    args = (o0, o1, om, lam_p, subln, wuv, w1, w2, res)
    return pl.pallas_call(
        functools.partial(_sample_out_kernel, li=li),
        grid=(1,),
        in_specs=[_full(a.shape) for a in args],
        out_specs=_full(res.shape),
        out_shape=jax.ShapeDtypeStruct(res.shape, F32),
        compiler_params=_params(1), name="sample_out")(*args)


def _rope_table(pos):
    half = MLA_ROPE // 2
    freq = ROPE_BASE ** (-jnp.arange(half, dtype=F32) / half)
    ang = pos.astype(F32)[:, None] * freq[None, :]
    c, s = jnp.cos(ang), jnp.sin(ang)
    return jnp.concatenate([c, c, -s, s], axis=-1)


def _rel_bucket(n):
    exact = REL_BUCKETS // 2
    nf = jnp.maximum(n, 1).astype(F32)
    large = exact + (jnp.log(nf / exact) / math.log(REL_MAX_DIST / exact)
                     * (REL_BUCKETS - exact)).astype(jnp.int32)
    return jnp.where(n < exact, n, jnp.minimum(large, REL_BUCKETS - 1))


def _row2(x):
    return x.reshape(1, -1)


def _diff_tiles(seq):
    tb = min(seq, 512)
    return tb, tb


def _mla_tiles(seq):
    tb = min(seq, 1024)
    return tb, min(tb, 256)


def kernel(x_prompt, x_sample, state_pool, cache_diff_k, cache_diff_v, cache_mla_ckv, cache_mla_kpe, page_table, norm_mix, norm_ffn, norm_final, w_ffn_up, w_ffn_down, w_in_ab, pool_w, pool_scale, sg_norm, sg_w, sg_b, w_out_ab, w_in_cd, diff_lambda, diff_subln, mla_q_norm, mla_w_qb, mla_kv_norm, mla_w_kvb, w_out_cd, rel_table):
    batch, seq, _ = x_prompt.shape
    nb, dec_seq, _ = x_sample.shape
    assert dec_seq == 1
    n_pages = page_table.shape[1]
    past_len = n_pages * PAGE_SIZE
    depth = norm_mix.shape[0]
    n_odd = cache_diff_k.shape[1]
    hp = x_prompt.reshape(batch * seq, D_MODEL)
    hs = x_sample.reshape(nb, D_MODEL)
    gf = _row2(norm_final)
    wu_all, wd_all = w_ffn_up.astype(BF16), w_ffn_down.astype(BF16)

    pool_p, pool_s, sgv_s, rows_p, rows_s = [], [], [], [], []
    for li in range(depth):
        e = li // 2
        g_mix, g_ffn = _row2(norm_mix[li]), _row2(norm_ffn[li])
        if li % 2 == 0:
            w_in = w_in_ab[e].astype(BF16)
            pw = pool_w[e].astype(BF16)
            ps = _row2(pool_scale[e])
            sgn = _row2(sg_norm[e])
            w_out = w_out_ab[e].astype(BF16)
            zuv = _ab_in(hp, g_mix, w_in, sgn)
            yab = _ab_mix(zuv, batch, seq, pw, ps, sg_w[e], sg_b[e, :, :SG_CHUNK].T)
            pool_p.append(zuv.reshape(batch, seq, -1)[:, seq - POOL_BUF:, :POOL_WIDTH])
            hp = _mm_res([yab], [w_out], hp)
            zuv_s = _ab_in(hs, g_mix, w_in, sgn)
            w0 = _row2(jnp.repeat(sg_w[e, :, 0, 0], SG_GW))
            b0 = _row2(jnp.repeat(sg_b[e, :, 0], SG_GW))
            cnts = tuple(float(min(past_len + 1, w)) for w in POOL_WINDOWS)
            yab_s = _ab_mix_s(zuv_s, jnp.transpose(state_pool[e], (1, 0, 2)), pw, ps, w0, b0, cnts)
            pool_s.append(jnp.concatenate([state_pool[e][:, 1:], zuv_s[:, None, :POOL_WIDTH]], axis=1))
            sgv_s.append(zuv_s[:, None, POOL_WIDTH + SG_WIDTH:])
            hs = _mm_res([yab_s], [w_out], hs)
        else:
            w = w_in_cd[e]
            wq = w[:, :CD_Q].reshape(D_MODEL, DIFF_KV_HEADS, 2, DIFF_GROUP, DIFF_DH)
            wq = jnp.transpose(wq, (0, 1, 3, 2, 4)).reshape(D_MODEL, CD_Q)
            half = MLA_ROPE // 2
            w_pe = w[:, -MLA_ROPE:]
            w_in = jnp.concatenate([wq, w[:, CD_Q:], w_pe[:, half:], w_pe[:, :half]], axis=1).astype(BF16)
            wqb = mla_w_qb[e].reshape(MLA_Q_LORA, MLA_HEADS, MLA_NOPE + MLA_ROPE)
            wq_pe = wqb[..., MLA_NOPE:]
            wq_pe = jnp.concatenate([wq_pe, wq_pe[..., half:], wq_pe[..., :half]], axis=-1)
            wqb = jnp.concatenate([wqb[..., :MLA_NOPE].reshape(MLA_Q_LORA, -1),
                                   wq_pe.reshape(MLA_Q_LORA, -1)], axis=1).astype(BF16)
            wkvb = mla_w_kvb[e].astype(BF16)
            wkvb_h = mla_w_kvb[e].reshape(MLA_KV_LORA, MLA_HEADS, MLA_NOPE + MLA_VD)
            w_uk_t = jnp.transpose(wkvb_h[..., :MLA_NOPE], (1, 2, 0)).astype(BF16)
            w_uv = jnp.transpose(wkvb_h[..., MLA_NOPE:], (1, 0, 2)).astype(BF16)
            w_out = w_out_cd[e].astype(BF16)
            w_out1, w_out2 = w_out[:DIFF_HEADS * DIFF_VD], w_out[DIFF_HEADS * DIFF_VD:]
            qn, kvn = _row2(mla_q_norm[e]), _row2(mla_kv_norm[e])
            lam_p, subln = diff_lambda[e], _row2(diff_subln[e])

            tb_d, rb_d = _diff_tiles(seq)
            tb_m, rb_m = _mla_tiles(seq)
            dist = (jnp.arange(rb_d)[:, None] - jnp.arange(rb_d + REL_MAX_DIST)[None, :]) + REL_MAX_DIST
            n_keys = past_len + LANES
            kpos = jnp.arange(n_keys)
            dist_s = jnp.where(kpos < past_len, past_len - kpos, 0)
            bias_near, bias_s = _bias_tiles(
                rel_table, dist, _rel_bucket(jnp.maximum(dist, 0)),
                jnp.broadcast_to(_rel_bucket(dist_s)[None], (8, n_keys)))

            cs = _rope_table(jnp.arange(seq))
            (qd, dk, dv, dkb, dvx, ckv, kpe, qm, km, vmx) = _cd_in(
                hp, g_mix, w_in, qn, wqb, kvn, wkvb, cs, batch, seq, False)
            yd = _diff_attn(qd, dkb, dvx, bias_near, lam_p, subln, batch, seq, tb_d, rb_d, li)
            ym = _mla_attn(qm, km, vmx, batch, seq, tb_m, rb_m)
            hp = _mm_res([yd, ym], [w_out1, w_out2], hp)
            rows_p.append((dk.reshape(batch, seq, DIFF_KV_HEADS, 2 * DIFF_DH),
                           dv.reshape(batch, seq, DIFF_KV_HEADS, DIFF_VD),
                           ckv.reshape(batch, seq, MLA_KV_LORA), kpe.reshape(batch, seq, MLA_ROPE)))

            cs_s = jnp.broadcast_to(_rope_table(jnp.full((1,), past_len)), (nb, LANES))
            (qd_s, dk_s, dv_s, ckv_s, kpe_s, qlat, qpe) = _cd_in(
                hs, g_mix, w_in, qn, wqb, kvn, w_uk_t, cs_s, nb, 1, True)
            row_id = jnp.arange(DIFF_MAPS)
            qd_r = jnp.transpose(qd_s.reshape(nb, DIFF_KV_HEADS, DIFF_GROUP, 2, DIFF_DH), (0, 1, 3, 2, 4))
            qd_r = qd_r.reshape(nb, DIFF_MAPS, 1, DIFF_DH)
            slot = ((row_id // DIFF_GROUP) % 2)[:, None] == jnp.arange(2)[None, :]
            qd_pad = jnp.where(slot[None, :, :, None], qd_r, 0.0).reshape(nb, DIFF_MAPS, LANES).astype(BF16)
            pad = ((0, 0), (0, DIFF_MAPS - MLA_HEADS), (0, 0))
            ql = jnp.pad(qlat.reshape(nb, MLA_HEADS, MLA_KV_LORA), pad)
            qp = jnp.pad(qpe.reshape(nb, MLA_HEADS, LANES)[:, :, :MLA_ROPE], pad).astype(BF16)
            row_head = row_id // (2 * DIFF_GROUP)
            bias_rows = bias_s[row_head * DIFF_GROUP + row_id % DIFF_GROUP, 0]
            own = row_head[:, None] == (jnp.arange(past_len * DIFF_KV_HEADS) % DIFF_KV_HEADS)[None, :]
            bias_past = jnp.where(own, jnp.repeat(bias_rows[:, :past_len], DIFF_KV_HEADS, axis=1), NEG_INF)
            od, om = _decode(
                page_table.reshape(-1) * n_odd + e, qd_pad, ql, qp, bias_past, bias_rows[:, past_len:],
                dk_s[:, None], dv_s[:, None], ckv_s[:, None], kpe_s[:, None],
                cache_diff_k.reshape(-1, LANES), cache_diff_v.reshape(-1, LANES),
                cache_mla_ckv.reshape(-1, PAGE_SIZE, MLA_KV_LORA),
                jnp.swapaxes(cache_mla_kpe, 2, 3).reshape(-1, MLA_ROPE, PAGE_SIZE), n_pages)
            od = od.reshape(nb, DIFF_KV_HEADS, 2, DIFF_GROUP * DIFF_VD)
            o_map = [od[:, :, mp].reshape(nb, -1) for mp in range(2)]
            hs = _sample_out(o_map[0], o_map[1], om[:, :MLA_HEADS].reshape(nb, -1), lam_p, subln, w_uv,
                             w_out1, w_out2, hs, li)
            rows_s.append((dk_s.reshape(nb, 1, DIFF_KV_HEADS, 2 * DIFF_DH),
                           dv_s.reshape(nb, 1, DIFF_KV_HEADS, DIFF_VD),
                           ckv_s.reshape(nb, 1, MLA_KV_LORA), kpe_s.reshape(nb, 1, MLA_ROPE)))
        last = li == depth - 1
        hp = _ffn(hp, g_ffn, wu_all, wd_all, li, gf, last)
        hs = _ffn(hs, g_ffn, wu_all, wd_all, li, gf, last)

    y_prompt = hp.reshape(batch, seq, D_MODEL)
    y_sample = hs.reshape(nb, 1, D_MODEL)
    stack = lambda rows, i: jnp.stack([r[i] for r in rows], axis=1)
    return (y_prompt, y_sample, jnp.stack(pool_p, axis=0), jnp.stack(pool_s, axis=0), jnp.stack(sgv_s, axis=0),
            stack(rows_p, 0), stack(rows_p, 1), stack(rows_p, 2), stack(rows_p, 3),
            stack(rows_s, 0), stack(rows_s, 1), stack(rows_s, 2), stack(rows_s, 3))
```
